```python
import math
import jax, jax.numpy as jnp
from jax import lax
import numpy as np

D_MODEL = 2048
BATCH = 1
SEQ = 16384
DEPTH = 2

MIX_WIDTH = D_MODEL
POOL_WIDTH = D_MODEL // 4
POOL_WINDOWS = (2, 4, 8, 16)
POOL_GROUP = POOL_WIDTH // len(POOL_WINDOWS)
LRU_WIDTH = D_MODEL // 4
LRU_HEADS = 8
LRU_HEAD_DIM = LRU_WIDTH // LRU_HEADS
LRU_CONV = 4
LRU_C = 8.0
ATTN_WIDTH = D_MODEL // 2
ATTN_HEAD_DIM = 64
ATTN_HEADS = ATTN_WIDTH // ATTN_HEAD_DIM
ATTN_BRANCHES = ((128, 1), (512, 4), (2048, 16))
ATTN_BLOCK = 128
D_FF = 5632
IN_WIDTH = POOL_WIDTH + 2 * LRU_WIDTH + 3 * ATTN_WIDTH
IN_SPLITS = (POOL_WIDTH, POOL_WIDTH + LRU_WIDTH, POOL_WIDTH + 2 * LRU_WIDTH,
             POOL_WIDTH + 2 * LRU_WIDTH + ATTN_WIDTH,
             POOL_WIDTH + 2 * LRU_WIDTH + 2 * ATTN_WIDTH)
NORM_EPS = 1e-6
NEG_INF = -1e30

kernel_name = 'hybrid_pool_rglru_dilated_attn_block'


def rms_norm(x, g):
    xf = x.astype(jnp.float32)
    y = xf * lax.rsqrt(jnp.mean(xf * xf, axis=-1, keepdims=True) + NORM_EPS)
    return (y * g.astype(jnp.float32)).astype(x.dtype)


def swiglu(x, w_in, w_out):
    gate, up = jnp.split(x @ w_in, 2, axis=-1)
    return (jax.nn.silu(gate) * up) @ w_out


def alibi_slopes(n_heads):
    return jnp.asarray(2.0 ** (-8.0 * np.arange(1, n_heads + 1) / n_heads), dtype=jnp.float32)


def pool_mixer(u, w, scale):
    b, s, _ = u.shape
    uf = u.astype(jnp.float32).reshape(b, s, len(POOL_WINDOWS), POOL_GROUP)
    cs = jnp.cumsum(uf, axis=1)
    pos = jnp.arange(1, s + 1, dtype=jnp.float32)
    means = []
    for gi, win in enumerate(POOL_WINDOWS):
        c = cs[:, :, gi]
        lag = jnp.pad(c, ((0, 0), (win, 0), (0, 0)))[:, :s]
        means.append((c - lag) / jnp.minimum(pos, float(win))[:, None])
    pooled = jnp.stack(means, axis=2) - uf
    y = jnp.einsum('bsgc,gcd->bsgd', pooled, w.astype(jnp.float32))
    y = y.reshape(b, s, POOL_WIDTH) * scale.astype(jnp.float32)
    return y.astype(u.dtype)


def _lru_combine(left, right):
    a1, b1 = left
    a2, b2 = right
    return a1 * a2, a2 * b1 + b2


def rg_lru_mixer(xb, gate_in, conv_w, conv_b, gate_w, gate_b, lam):
    b, s, c = xb.shape
    xc = lax.conv_general_dilated(
        xb, conv_w[:, None, :], window_strides=(1,), padding=((LRU_CONV - 1, 0),),
        dimension_numbers=('NWC', 'WIO', 'NWC'), feature_group_count=c) + conv_b
    xh = xc.reshape(b, s, LRU_HEADS, LRU_HEAD_DIM)
    g = jnp.einsum('bshc,ghcd->gbshd', xh, gate_w) + gate_b[:, None, None]
    r = jax.nn.sigmoid(g[0].astype(jnp.float32)).reshape(b, s, c)
    i = jax.nn.sigmoid(g[1].astype(jnp.float32)).reshape(b, s, c)
    log_a = -LRU_C * r * jax.nn.softplus(-lam.astype(jnp.float32))
    a = jnp.exp(log_a)
    u = jnp.sqrt(-jnp.expm1(2.0 * log_a)) * (i * xc.astype(jnp.float32))
    _, h = lax.associative_scan(_lru_combine, (a, u), axis=1)
    y = h * jax.nn.gelu(gate_in.astype(jnp.float32))
    return y.astype(xb.dtype)


def _dilated_branch(q, k, v, window, dilation, slopes):
    b, s, h, dh = q.shape
    span = window // dilation
    unit = dilation * ATTN_BLOCK
    s_pad = -(-s // unit) * unit
    n_sub = s_pad // dilation
    nb = n_sub // ATTN_BLOCK
    pad = ((0, 0), (0, s_pad - s), (0, 0), (0, 0))

    def to_blocks(t):
        t = jnp.pad(t, pad).reshape(b, n_sub, dilation, h, dh)
        t = jnp.transpose(t, (0, 2, 1, 3, 4))
        return t.reshape(b, dilation, nb, ATTN_BLOCK, h, dh)

    def with_prev(t):
        prev = jnp.pad(t, ((0, 0), (0, 0), (1, 0), (0, 0), (0, 0), (0, 0)))[:, :, :-1]
        return jnp.concatenate([prev, t], axis=3)

    qb = to_blocks(q)
    kb = with_prev(to_blocks(k))
    vb = with_prev(to_blocks(v)).astype(jnp.float32)
    scores = jnp.einsum('brnqhc,brnkhc->brnhqk', qb, kb).astype(jnp.float32)
    qi = jnp.arange(ATTN_BLOCK)[:, None]
    ki = jnp.arange(2 * ATTN_BLOCK)[None, :]
    dist = qi + ATTN_BLOCK - ki
    first = (jnp.arange(nb) == 0)[:, None, None] & (ki < ATTN_BLOCK)[None]
    valid = ((dist >= 0) & (dist <= span))[None] & ~first
    bias = -slopes[:, None, None] * (dist * dilation).astype(jnp.float32)[None]
    scores = jnp.where(valid[None, None, :, None], scores + bias, NEG_INF)
    m = jnp.max(scores, axis=-1)
    p = jnp.exp(scores - m[..., None])
    l = jnp.sum(p, axis=-1)
    m = jnp.transpose(m, (0, 1, 2, 4, 3))
    l = jnp.transpose(l, (0, 1, 2, 4, 3))
    o = jnp.einsum('brnhqk,brnkhc->brnqhc', p, vb) / l[..., None]

    def from_blocks(t):
        tail = t.shape[5:]
        t = t.reshape((b, dilation, n_sub, h) + tail)
        t = jnp.moveaxis(t, 1, 2)
        return t.reshape((b, s_pad, h) + tail)[:, :s]

    return from_blocks(o), from_blocks(m), from_blocks(l)


def dilated_attention(q, k, v):
    slopes = alibi_slopes(q.shape[2])
    outs, ms, ls = [], [], []
    for window, dilation in ATTN_BRANCHES:
        o, m, l = _dilated_branch(q, k, v, window, dilation, slopes)
        outs.append(o)
        ms.append(m)
        ls.append(l)
    m_all = jnp.stack(ms)
    wts = jnp.stack(ls) * jnp.exp(m_all - jnp.max(m_all, axis=0, keepdims=True))
    o = jnp.sum(wts[..., None] * jnp.stack(outs), axis=0) / jnp.sum(wts, axis=0)[..., None]
    return o.astype(q.dtype)


def hybrid_mixer(h, w_in, w_out, pool_w, pool_scale, conv_w, conv_b, gate_w, gate_b, lam):
    b, s, _ = h.shape
    z = h @ w_in
    pool_in, lru_x, lru_g, q, k, v = jnp.split(z, IN_SPLITS, axis=-1)
    y_pool = pool_mixer(pool_in, pool_w, pool_scale)
    y_lru = rg_lru_mixer(lru_x, lru_g, conv_w, conv_b, gate_w, gate_b, lam)
    q = q.reshape(b, s, ATTN_HEADS, ATTN_HEAD_DIM) * (ATTN_HEAD_DIM ** -0.5)
    k = k.reshape(b, s, ATTN_HEADS, ATTN_HEAD_DIM)
    v = v.reshape(b, s, ATTN_HEADS, ATTN_HEAD_DIM)
    y_attn = dilated_attention(q, k, v).reshape(b, s, ATTN_WIDTH)
    y = jnp.concatenate([y_pool, y_lru.astype(h.dtype), y_attn.astype(h.dtype)], axis=-1)
    return y @ w_out


def setup_inputs(seed: int = 0) -> dict:
    key = jax.random.key(seed)
    ks = jax.random.split(key, 16)
    f32 = jnp.float32

    def nrm(k, shape, scale):
        return jax.random.normal(k, shape, f32) * scale

    a0 = jax.random.uniform(ks[15], (DEPTH, LRU_WIDTH), f32, 0.9, 0.999)
    p0 = a0 ** (1.0 / LRU_C)
    lam = jnp.log(p0) - jnp.log1p(-p0)
    return {
        'x': nrm(ks[0], (BATCH, SEQ, D_MODEL), 1.0),
        'norm_g': 1.0 + nrm(ks[1], (DEPTH, 6, D_MODEL), 0.1),
        'ffn1_w_in': nrm(ks[2], (DEPTH, D_MODEL, 2 * D_FF), D_MODEL ** -0.5),
        'ffn1_w_out': nrm(ks[3], (DEPTH, D_FF, D_MODEL), D_FF ** -0.5),
        'mix_w_in': nrm(ks[4], (DEPTH, D_MODEL, IN_WIDTH), D_MODEL ** -0.5),
        'mix_w_out': nrm(ks[5], (DEPTH, MIX_WIDTH, D_MODEL), MIX_WIDTH ** -0.5),
        'pool_w': nrm(ks[6], (DEPTH, len(POOL_WINDOWS), POOL_GROUP, POOL_GROUP), POOL_GROUP ** -0.5),
        'pool_scale': 1.0 + nrm(ks[7], (DEPTH, POOL_WIDTH), 0.1),
        'lru_conv_w': nrm(ks[8], (DEPTH, LRU_CONV, LRU_WIDTH), LRU_CONV ** -0.5),
        'lru_conv_b': nrm(ks[9], (DEPTH, LRU_WIDTH), 0.1),
        'lru_gate_w': nrm(ks[10], (DEPTH, 2, LRU_HEADS, LRU_HEAD_DIM, LRU_HEAD_DIM), LRU_HEAD_DIM ** -0.5),
        'lru_gate_b': nrm(ks[11], (DEPTH, 2, LRU_HEADS, LRU_HEAD_DIM), 0.1),
        'lru_lambda': lam,
        'ffn2_w_in': nrm(ks[12], (DEPTH, D_MODEL, 2 * D_FF), D_MODEL ** -0.5),
        'ffn2_w_out': nrm(ks[13], (DEPTH, D_FF, D_MODEL), D_FF ** -0.5),
    }


def reference(x, norm_g, ffn1_w_in, ffn1_w_out, mix_w_in, mix_w_out, pool_w, pool_scale,
              lru_conv_w, lru_conv_b, lru_gate_w, lru_gate_b, lru_lambda, ffn2_w_in, ffn2_w_out):
    for layer in range(DEPTH):
        g = norm_g[layer]
        h = swiglu(rms_norm(x, g[0]), ffn1_w_in[layer], ffn1_w_out[layer])
        x = x + 0.5 * rms_norm(h, g[1])
        h = hybrid_mixer(rms_norm(x, g[2]), mix_w_in[layer], mix_w_out[layer], pool_w[layer],
                         pool_scale[layer], lru_conv_w[layer], lru_conv_b[layer],
                         lru_gate_w[layer], lru_gate_b[layer], lru_lambda[layer])
        x = x + rms_norm(h, g[3])
        h = swiglu(rms_norm(x, g[4]), ffn2_w_in[layer], ffn2_w_out[layer])
        x = x + 0.5 * rms_norm(h, g[5])
    return x
```

```python
import functools
import math

import jax
import jax.numpy as jnp
from jax import lax
from jax.experimental import pallas as pl
from jax.experimental.pallas import tpu as pltpu

F32 = jnp.float32
BF16 = jnp.bfloat16

D_MODEL = 2048
SEQ = 16384
DEPTH = 2
D_FF = 5632
POOL_WIDTH = 512
POOL_WINDOWS = (2, 4, 8, 16)
POOL_GROUP = 128
LRU_WIDTH = 512
LRU_HEADS = 8
LRU_HEAD_DIM = 64
LRU_CONV = 4
LRU_C = 8.0
ATTN_WIDTH = 1024
ATTN_HEAD_DIM = 64
ATTN_HEADS = 16
ATTN_BRANCHES = ((128, 1), (512, 4), (2048, 16))
ATTN_BLOCK = 128
PL_WIDTH = POOL_WIDTH + 2 * LRU_WIDTH
IN_WIDTH = PL_WIDTH + 3 * ATTN_WIDTH
NORM_EPS = 1e-6
NEG_INF = -1e30

V7X_LANES = 128
V7X_VMEM_BYTES = 64 * 1024 * 1024
VMEM_LIMIT_CAP = 60000 * 1024

FFN_TM = 1024
FFN_TF = 512
PROJ_TM = 1024
PROJ_TN = 512
OUT_TM = 512
MIX_T = 512
MIX_HALO = 16
ATTN_TILE = 2048
ATTN_W = 512
ATTN_SUB = ATTN_TILE // ATTN_BLOCK
HEAD_PAIRS = ATTN_W // V7X_LANES
COMBINE_ROWS = 256
NORM_ROWS = 128


def _vmem_limit(nbytes):
    return int(min(VMEM_LIMIT_CAP, nbytes))


def _rms_scale(x):
    return lax.rsqrt(jnp.mean(x * x, axis=-1, keepdims=True) + NORM_EPS)


def _for_row_chunks(n_rows, body):
    def step(c, carry):
        body(pl.ds(pl.multiple_of(c * NORM_ROWS, NORM_ROWS), NORM_ROWS))
        return carry
    lax.fori_loop(0, n_rows // NORM_ROWS, step, 0)


def _norm_to_bf16(x_ref, g_ref, xn_ref):
    def body(rows):
        x = x_ref[rows, :]
        xn_ref[rows, :] = ((x * _rms_scale(x)) * g_ref[...]).astype(BF16)
    _for_row_chunks(x_ref.shape[0], body)


def _ffn_kernel(x_ref, gin_ref, gout_ref, wg_ref, wu_ref, wo_ref, o_ref, xn_ref):
    f = pl.program_id(1)

    @pl.when(f == 0)
    def _():
        _norm_to_bf16(x_ref, gin_ref, xn_ref)
        o_ref[...] = jnp.zeros_like(o_ref)

    xn = xn_ref[...]
    gate = jnp.dot(xn, wg_ref[...], preferred_element_type=F32)
    up = jnp.dot(xn, wu_ref[...], preferred_element_type=F32)
    h = ((gate * jax.nn.sigmoid(gate)) * up).astype(BF16)
    o_ref[...] += jnp.dot(h, wo_ref[...], preferred_element_type=F32)

    @pl.when(f == pl.num_programs(1) - 1)
    def _():
        def body(rows):
            acc = o_ref[rows, :]
            o_ref[rows, :] = x_ref[rows, :] + 0.5 * ((acc * _rms_scale(acc)) * gout_ref[...])
        _for_row_chunks(o_ref.shape[0], body)


def _ffn(x, g_in, g_out, w_in, w_out, layer):
    nf = D_FF // FFN_TF
    vmem = (4 * FFN_TM * D_MODEL * 4
            + FFN_TM * D_MODEL * 2
            + 2 * 3 * D_MODEL * FFN_TF * 2
            + 8 * FFN_TM * FFN_TF * 4)
    return pl.pallas_call(
        _ffn_kernel,
        out_shape=jax.ShapeDtypeStruct((SEQ, D_MODEL), F32),
        grid=(SEQ // FFN_TM, nf),
        in_specs=[
            pl.BlockSpec((FFN_TM, D_MODEL), lambda i, f: (i, 0)),
            pl.BlockSpec((1, D_MODEL), lambda i, f: (0, 0)),
            pl.BlockSpec((1, D_MODEL), lambda i, f: (0, 0)),
            pl.BlockSpec((None, D_MODEL, FFN_TF), lambda i, f: (layer, 0, f)),
            pl.BlockSpec((None, D_MODEL, FFN_TF), lambda i, f: (layer, 0, f + nf)),
            pl.BlockSpec((None, FFN_TF, D_MODEL), lambda i, f: (layer, f, 0)),
        ],
        out_specs=pl.BlockSpec((FFN_TM, D_MODEL), lambda i, f: (i, 0)),
        scratch_shapes=[pltpu.VMEM((FFN_TM, D_MODEL), BF16)],
        compiler_params=pltpu.CompilerParams(
            dimension_semantics=("parallel", "arbitrary"),
            vmem_limit_bytes=_vmem_limit(vmem)),
        name="ffn",
    )(x, g_in, g_out, w_in, w_in, w_out)


PL_TILES = PL_WIDTH // PROJ_TN


def _inproj_kernel(x_ref, g_ref, w_ref, zpl_ref, qkv_ref, xn_ref):
    n = pl.program_id(1)

    @pl.when(n == 0)
    def _():
        _norm_to_bf16(x_ref, g_ref, xn_ref)

    z = jnp.dot(xn_ref[...], w_ref[...], preferred_element_type=F32)

    @pl.when(n < PL_TILES)
    def _():
        zpl_ref[...] = z

    @pl.when(n >= PL_TILES)
    def _():
        qkv_ref[...] = z.astype(BF16)


def _inproj(x, g, w_in, layer):
    vmem = (2 * PROJ_TM * D_MODEL * 4 + PROJ_TM * D_MODEL * 2
            + 2 * D_MODEL * PROJ_TN * 2 + 8 * PROJ_TM * PROJ_TN * 4)
    return pl.pallas_call(
        _inproj_kernel,
        out_shape=(jax.ShapeDtypeStruct((SEQ, PL_WIDTH), F32),
                   jax.ShapeDtypeStruct((SEQ, 3 * ATTN_WIDTH), BF16)),
        grid=(SEQ // PROJ_TM, IN_WIDTH // PROJ_TN),
        in_specs=[
            pl.BlockSpec((PROJ_TM, D_MODEL), lambda i, n: (i, 0)),
            pl.BlockSpec((1, D_MODEL), lambda i, n: (0, 0)),
            pl.BlockSpec((None, D_MODEL, PROJ_TN), lambda i, n: (layer, 0, n)),
        ],
        out_specs=(
            pl.BlockSpec((PROJ_TM, PROJ_TN), lambda i, n: (i, jnp.minimum(n, PL_TILES - 1))),
            pl.BlockSpec((PROJ_TM, PROJ_TN), lambda i, n: (i, jnp.maximum(n - PL_TILES, 0))),
        ),
        scratch_shapes=[pltpu.VMEM((PROJ_TM, D_MODEL), BF16)],
        compiler_params=pltpu.CompilerParams(
            dimension_semantics=("parallel", "arbitrary"),
            vmem_limit_bytes=_vmem_limit(vmem)),
        name="mix_inproj",
    )(x, g, w_in)


def _shift_rows(v, k, fill):
    t = v.shape[0]
    if k % 8 == 0:
        pad = jnp.full((k, v.shape[1]), fill, v.dtype)
        return jnp.concatenate([pad, v[: t - k]], axis=0)
    rolled = pltpu.roll(v, k, 0)
    row = lax.broadcasted_iota(jnp.int32, v.shape, 0)
    return jnp.where(row >= k, rolled, fill)


def _mix_kernel(z_ref, pw_ref, ps_ref, cw_ref, cb_ref, wr_ref, wi_ref, br_ref, bi_ref,
                lam_ref, y_ref, ext_ref, h_ref):
    i = pl.program_id(0)
    T = MIX_T
    H = MIX_HALO

    @pl.when(i == 0)
    def _():
        ext_ref[0:H, :] = jnp.zeros((H, POOL_WIDTH + LRU_WIDTH), F32)
        h_ref[...] = jnp.zeros_like(h_ref)

    ext_ref[H:H + T, :] = z_ref[:, 0:POOL_WIDTH + LRU_WIDTH]

    pos = (i * T + 1 + lax.broadcasted_iota(jnp.int32, (T, POOL_GROUP), 0)).astype(F32)
    for g, win in enumerate(POOL_WINDOWS):
        lanes = slice(g * POOL_GROUP, (g + 1) * POOL_GROUP)
        u = ext_ref[H:H + T, lanes]
        s = u
        for k in range(1, win):
            s = s + ext_ref[H - k:H - k + T, lanes]
        pooled = s / jnp.minimum(pos, float(win)) - u
        y = jnp.dot(pooled.astype(BF16), pw_ref[g], preferred_element_type=F32)
        y_ref[:, lanes] = (y * ps_ref[:, lanes]).astype(BF16)

    xl = slice(POOL_WIDTH, POOL_WIDTH + LRU_WIDTH)
    xc = cb_ref[...] + cw_ref[LRU_CONV - 1:LRU_CONV, :] * ext_ref[H:H + T, xl]
    for k in range(1, LRU_CONV):
        xc = xc + cw_ref[LRU_CONV - 1 - k:LRU_CONV - k, :] * ext_ref[H - k:H - k + T, xl]
    xcb = xc.astype(BF16)
    r = jax.nn.sigmoid(jnp.dot(xcb, wr_ref[...], preferred_element_type=F32) + br_ref[...])
    ig = jax.nn.sigmoid(jnp.dot(xcb, wi_ref[...], preferred_element_type=F32) + bi_ref[...])
    nlam = -lam_ref[...]
    softplus = jnp.maximum(nlam, 0.0) + jnp.log1p(jnp.exp(-jnp.abs(nlam)))
    log_a = (-LRU_C * r) * softplus
    a = jnp.exp(log_a)
    b = jnp.sqrt(-jnp.tanh(log_a) * (a * a + 1.0)) * (ig * xc)
    k = 1
    while k < T:
        b = a * _shift_rows(b, k, 0.0) + b
        a = a * _shift_rows(a, k, 1.0)
        k *= 2
    h = a * h_ref[0:1, :] + b
    h_ref[0:1, :] = h[T - 1:T, :]
    gate_in = z_ref[:, POOL_WIDTH + LRU_WIDTH:PL_WIDTH]
    y_ref[:, POOL_WIDTH:POOL_WIDTH + LRU_WIDTH] = (h * jax.nn.gelu(gate_in)).astype(BF16)

    ext_ref[0:H, :] = ext_ref[T:T + H, :]


def _mix(zpl, pool_w, pool_scale, conv_w, conv_b, w_r, w_i, b_r, b_i, lam):
    full = lambda shape: pl.BlockSpec(shape, lambda i: (0,) * len(shape))
    vmem = (2 * MIX_T * PL_WIDTH * 4 + 2 * MIX_T * 1024 * 2 + (MIX_T + MIX_HALO) * 1024 * 4
            + 4 * 512 * 512 * 2 + 24 * MIX_T * LRU_WIDTH * 4)
    return pl.pallas_call(
        _mix_kernel,
        out_shape=jax.ShapeDtypeStruct((SEQ, POOL_WIDTH + LRU_WIDTH), BF16),
        grid=(SEQ // MIX_T,),
        in_specs=[
            pl.BlockSpec((MIX_T, PL_WIDTH), lambda i: (i, 0)),
            full((len(POOL_WINDOWS), POOL_GROUP, POOL_GROUP)),
            full((1, POOL_WIDTH)),
            full((LRU_CONV, LRU_WIDTH)),
            full((1, LRU_WIDTH)),
            full((LRU_WIDTH, LRU_WIDTH)),
            full((LRU_WIDTH, LRU_WIDTH)),
            full((1, LRU_WIDTH)),
            full((1, LRU_WIDTH)),
            full((1, LRU_WIDTH)),
        ],
        out_specs=pl.BlockSpec((MIX_T, POOL_WIDTH + LRU_WIDTH), lambda i: (i, 0)),
        scratch_shapes=[pltpu.VMEM((MIX_T + MIX_HALO, POOL_WIDTH + LRU_WIDTH), F32),
                        pltpu.VMEM((8, LRU_WIDTH), F32)],
        compiler_params=pltpu.CompilerParams(
            dimension_semantics=("arbitrary",),
            vmem_limit_bytes=_vmem_limit(vmem)),
        name="pool_lru",
    )(zpl, pool_w, pool_scale, conv_w, conv_b, w_r, w_i, b_r, b_i, lam)


def _attn_kernel(q1, kc1, kp1, vc1, vp1, q4, kc4, kp4, vc4, vp4, q16, kc16, kp16, vc16, vp16,
                 y_ref, o_s, l_s, bias_s):
    hg = pl.program_id(0)
    j = pl.program_id(1)
    s = pl.program_id(2)
    B = ATTN_BLOCK
    lane = lax.broadcasted_iota(jnp.int32, (B, V7X_LANES), 1)
    low = lane < ATTN_HEAD_DIM

    @pl.when(jnp.logical_and(j == 0, s == 0))
    def _():
        qi = lax.broadcasted_iota(jnp.int32, (B, B), 0)
        ki = lax.broadcasted_iota(jnp.int32, (B, B), 1)
        group_scale = jnp.where(hg == 0, 1.0, 2.0 ** (-8.0 * (ATTN_W // ATTN_HEAD_DIM) / ATTN_HEADS))
        for bi, (_, dil) in enumerate(ATTN_BRANCHES):
            dist_prev = ((qi - ki + B) * dil).astype(F32)
            dist_cur = ((qi - ki) * dil).astype(F32)
            for hl in range(ATTN_W // ATTN_HEAD_DIM):
                slope = group_scale * float(2.0 ** (-8.0 * (hl + 1) / ATTN_HEADS))
                bias_s[bi, hl, 0] = jnp.where(ki >= qi, (-slope) * dist_prev, NEG_INF)
                bias_s[bi, hl, 1] = jnp.where(ki <= qi, (-slope) * dist_cur, NEG_INF)

    branches = (
        (q1, kc1, kp1, vc1, vp1, 16 * j + s == 0, s * B, 1),
        (q4, kc4, kp4, vc4, vp4, 4 * j + s % 4 == 0, (s % 4) * (4 * B) + s // 4, 4),
        (q16, kc16, kp16, vc16, vp16, j == 0, s, 16),
    )
    dn = (((1,), (1,)), ((), ()))
    for bi, (q_ref, kc_ref, kp_ref, vc_ref, vp_ref, first, start, dil) in enumerate(branches):
        first_neg = jnp.where(first, NEG_INF, 0.0).astype(F32)
        for hp in range(HEAD_PAIRS):
            lanes = slice(hp * V7X_LANES, (hp + 1) * V7X_LANES)
            qp = q_ref[:, lanes]
            kp = kp_ref[:, lanes]
            kc = kc_ref[:, lanes]
            vp = vp_ref[:, lanes]
            vc = vc_ref[:, lanes]
            outs = []
            lses = []
            for e in range(2):
                qe = jnp.where(low if e == 0 else jnp.logical_not(low), qp, jnp.zeros_like(qp))
                sp = lax.dot_general(qe, kp, dn, preferred_element_type=F32)
                sc = lax.dot_general(qe, kc, dn, preferred_element_type=F32)
                sp = sp + (bias_s[bi, 2 * hp + e, 0] + first_neg)
                sc = sc + bias_s[bi, 2 * hp + e, 1]
                m = jnp.maximum(jnp.max(sp, axis=1, keepdims=True),
                                jnp.max(sc, axis=1, keepdims=True))
                pp = jnp.exp(sp - m)
                pc = jnp.exp(sc - m)
                l = jnp.sum(pp, axis=1, keepdims=True) + jnp.sum(pc, axis=1, keepdims=True)
                o = (jnp.dot(pp.astype(BF16), vp, preferred_element_type=F32)
                     + jnp.dot(pc.astype(BF16), vc, preferred_element_type=F32))
                outs.append(o / l)
                lses.append(jnp.broadcast_to(m + jnp.log(l), (B, V7X_LANES)))
            o_pair = jnp.where(low, outs[0], outs[1])
            l_pair = jnp.where(low, lses[0], lses[1])
            if dil == 1:
                rows = pl.ds(pl.multiple_of(start, B), B)
            else:
                rows = pl.ds(start, B, stride=dil)
            o_s[bi, hp, rows, :] = o_pair
            l_s[bi, hp, rows, :] = l_pair

    @pl.when(s == ATTN_SUB - 1)
    def _():
        def chunk(c, carry):
            rows = pl.ds(pl.multiple_of(c * COMBINE_ROWS, COMBINE_ROWS), COMBINE_ROWS)
            for hp in range(HEAD_PAIRS):
                l1 = l_s[0, hp, rows, :]
                l4 = l_s[1, hp, rows, :]
                l16 = l_s[2, hp, rows, :]
                top = jnp.maximum(jnp.maximum(l1, l4), l16)
                w1 = jnp.exp(l1 - top)
                w4 = jnp.exp(l4 - top)
                w16 = jnp.exp(l16 - top)
                num = w1 * o_s[0, hp, rows, :] + w4 * o_s[1, hp, rows, :] + w16 * o_s[2, hp, rows, :]
                y_ref[rows, hp * V7X_LANES:(hp + 1) * V7X_LANES] = (num / (w1 + w4 + w16)).astype(BF16)
            return carry
        lax.fori_loop(0, ATTN_TILE // COMBINE_ROWS, chunk, 0)


def _attention(qkv):
    B = ATTN_BLOCK
    ngrp = ATTN_WIDTH // ATTN_W
    per_pos = 3 * ngrp
    blk = lambda imap: pl.BlockSpec((B, ATTN_W), imap)

    def branch_specs(dil):
        def rowblk(j, s):
            if dil == 1:
                return 16 * j + s
            if dil == 4:
                return 4 * j + s % 4
            return j

        def res(s):
            if dil == 1:
                return 0
            if dil == 4:
                return s // 4
            return s

        def spec(which, prev):
            def imap(hg, j, s):
                rb = rowblk(j, s)
                if prev:
                    rb = jnp.maximum(rb - 1, 0)
                return (rb, res(s) * per_pos + which * ngrp + hg)
            return blk(imap)
        return [spec(0, False), spec(1, False), spec(1, True), spec(2, False), spec(2, True)]

    views = []
    specs = []
    for _, dil in ATTN_BRANCHES:
        view = qkv.reshape(SEQ // dil, dil * 3 * ATTN_WIDTH)
        views += [view] * 5
        specs += branch_specs(dil)
    nheads_step = ATTN_W // ATTN_HEAD_DIM
    vmem = (2 * 15 * B * ATTN_W * 2 + 2 * ATTN_TILE * ATTN_W * 2
            + 2 * 3 * HEAD_PAIRS * ATTN_TILE * V7X_LANES * 4
            + 3 * nheads_step * 2 * B * B * 4 + 8 * 1024 * 1024)
    return pl.pallas_call(
        _attn_kernel,
        out_shape=jax.ShapeDtypeStruct((SEQ, ATTN_WIDTH), BF16),
        grid=(ngrp, SEQ // ATTN_TILE, ATTN_SUB),
        in_specs=specs,
        out_specs=pl.BlockSpec((ATTN_TILE, ATTN_W), lambda hg, j, s: (j, hg)),
        scratch_shapes=[
            pltpu.VMEM((3, HEAD_PAIRS, ATTN_TILE, V7X_LANES), F32),
            pltpu.VMEM((3, HEAD_PAIRS, ATTN_TILE, V7X_LANES), F32),
            pltpu.VMEM((3, nheads_step, 2, B, B), F32),
        ],
        compiler_params=pltpu.CompilerParams(
            dimension_semantics=("arbitrary", "arbitrary", "arbitrary"),
            vmem_limit_bytes=_vmem_limit(vmem)),
        name="dilated_attn",
    )(*views)


def _outproj_kernel(x_ref, ypl_ref, ya_ref, g_ref, w_ref, o_ref):
    half = POOL_WIDTH + LRU_WIDTH

    def body(rows):
        h = (jnp.dot(ypl_ref[rows, :], w_ref[0:half, :], preferred_element_type=F32)
             + jnp.dot(ya_ref[rows, :], w_ref[half:2 * half, :], preferred_element_type=F32))
        o_ref[rows, :] = x_ref[rows, :] + (h * _rms_scale(h)) * g_ref[...]
    _for_row_chunks(o_ref.shape[0], body)


def _outproj(x, ypl, ya, g, w_out, layer):
    half = POOL_WIDTH + LRU_WIDTH
    vmem = (4 * OUT_TM * D_MODEL * 4 + 4 * OUT_TM * half * 2 + 2 * D_MODEL * D_MODEL * 2
            + 4 * OUT_TM * D_MODEL * 4)
    return pl.pallas_call(
        _outproj_kernel,
        out_shape=jax.ShapeDtypeStruct((SEQ, D_MODEL), F32),
        grid=(SEQ // OUT_TM,),
        in_specs=[
            pl.BlockSpec((OUT_TM, D_MODEL), lambda i: (i, 0)),
            pl.BlockSpec((OUT_TM, half), lambda i: (i, 0)),
            pl.BlockSpec((OUT_TM, ATTN_WIDTH), lambda i: (i, 0)),
            pl.BlockSpec((1, D_MODEL), lambda i: (0, 0)),
            pl.BlockSpec((None, D_MODEL, D_MODEL), lambda i: (layer, 0, 0)),
        ],
        out_specs=pl.BlockSpec((OUT_TM, D_MODEL), lambda i: (i, 0)),
        compiler_params=pltpu.CompilerParams(
            dimension_semantics=("parallel",),
            vmem_limit_bytes=_vmem_limit(vmem)),
        name="mix_outproj",
    )(x, ypl, ya, g, w_out)


def _block_diag(w):
    heads, c, _ = w.shape
    eye = jnp.eye(heads, dtype=w.dtype)
    return (eye[:, None, :, None] * w[:, :, None, :]).reshape(heads * c, heads * c)


def kernel(x, norm_g, ffn1_w_in, ffn1_w_out, mix_w_in, mix_w_out, pool_w, pool_scale,
           lru_conv_w, lru_conv_b, lru_gate_w, lru_gate_b, lru_lambda, ffn2_w_in, ffn2_w_out):
    xs = x.reshape(SEQ, D_MODEL)
    w1_in = ffn1_w_in.astype(BF16)
    w1_out = ffn1_w_out.astype(BF16)
    w2_in = ffn2_w_in.astype(BF16)
    w2_out = ffn2_w_out.astype(BF16)
    qscale = jnp.concatenate([jnp.ones((PL_WIDTH,), F32),
                              jnp.full((ATTN_WIDTH,), ATTN_HEAD_DIM ** -0.5, F32),
                              jnp.ones((2 * ATTN_WIDTH,), F32)])
    wm_in = (mix_w_in * qscale).astype(BF16)
    wm_out = mix_w_out.astype(BF16)
    pw = pool_w.astype(BF16)
    for layer in range(DEPTH):
        g = norm_g[layer].reshape(6, 1, D_MODEL)
        xs = _ffn(xs, g[0], g[1], w1_in, w1_out, layer)
        zpl, qkv = _inproj(xs, g[2], wm_in, layer)
        ypl = _mix(zpl, pw[layer], pool_scale[layer].reshape(1, POOL_WIDTH),
                   lru_conv_w[layer], lru_conv_b[layer].reshape(1, LRU_WIDTH),
                   _block_diag(lru_gate_w[layer, 0]).astype(BF16),
                   _block_diag(lru_gate_w[layer, 1]).astype(BF16),
                   lru_gate_b[layer, 0].reshape(1, LRU_WIDTH),
                   lru_gate_b[layer, 1].reshape(1, LRU_WIDTH),
                   lru_lambda[layer].reshape(1, LRU_WIDTH))
        ya = _attention(qkv)
        xs = _outproj(xs, ypl, ya, g[3], wm_out, layer)
        xs = _ffn(xs, g[4], g[5], w2_in, w2_out, layer)
    return xs.reshape(x.shape)
```

```python
import jax
import jax.numpy as jnp
from jax import lax
from jax.experimental import pallas as pl
from jax.experimental.pallas import tpu as pltpu

F32 = jnp.float32
BF16 = jnp.bfloat16

D_MODEL = 2048
SEQ = 16384
DEPTH = 2
D_FF = 5632
POOL_WIDTH = 512
POOL_WINDOWS = (2, 4, 8, 16)
POOL_GROUP = 128
LRU_WIDTH = 512
LRU_HEADS = 8
LRU_HEAD_DIM = 64
LRU_CONV = 4
LRU_C = 8.0
ATTN_WIDTH = 1024
ATTN_HEAD_DIM = 64
ATTN_HEADS = 16
ATTN_DILATIONS = (1, 4, 16)
ATTN_BLOCK = 128
PL_WIDTH = POOL_WIDTH + 2 * LRU_WIDTH
IN_WIDTH = PL_WIDTH + 3 * ATTN_WIDTH
NORM_EPS = 1e-6
NEG_INF = -1e30

V7X_LANES = 128
VMEM_LIMIT_CAP = 60000 * 1024

FFN_TM = 1024
FFN_TF = 512
PROJ_TM = 1024
PROJ_TN = 512
OUT_TM = 512
MIX_T = 512
MIX_HALO = 16
DIL_PLANES = 16
ATTN_TILE = DIL_PLANES * ATTN_BLOCK
ATTN_W = 256
HEAD_PAIRS = ATTN_W // V7X_LANES
ATTN_GROUP = 4
QUARTER = ATTN_BLOCK // 4
COMBINE_ROWS = 256
NORM_ROWS = 128


def _vmem_limit(nbytes):
    return int(min(VMEM_LIMIT_CAP, nbytes))


def _rms_scale(x):
    return lax.rsqrt(jnp.mean(x * x, axis=-1, keepdims=True) + NORM_EPS)


def _for_row_chunks(n_rows, body):
    def step(c, carry):
        body(pl.ds(pl.multiple_of(c * NORM_ROWS, NORM_ROWS), NORM_ROWS))
        return carry
    lax.fori_loop(0, n_rows // NORM_ROWS, step, 0)


def _norm_to_bf16(x_ref, g_ref, xn_ref):
    def body(rows):
        x = x_ref[rows, :]
        xn_ref[rows, :] = ((x * _rms_scale(x)) * g_ref[...]).astype(BF16)
    _for_row_chunks(x_ref.shape[0], body)


def _ffn_kernel(x_ref, gin_ref, gout_ref, wg_ref, wu_ref, wo_ref, o_ref, xn_ref):
    f = pl.program_id(1)

    @pl.when(f == 0)
    def _():
        _norm_to_bf16(x_ref, gin_ref, xn_ref)
        o_ref[...] = jnp.zeros_like(o_ref)

    xn = xn_ref[...]
    gate = jnp.dot(xn, wg_ref[...], preferred_element_type=F32)
    up = jnp.dot(xn, wu_ref[...], preferred_element_type=F32)
    h = ((gate * jax.nn.sigmoid(gate)) * up).astype(BF16)
    o_ref[...] += jnp.dot(h, wo_ref[...], preferred_element_type=F32)

    @pl.when(f == pl.num_programs(1) - 1)
    def _():
        def body(rows):
            acc = o_ref[rows, :]
            o_ref[rows, :] = x_ref[rows, :] + 0.5 * ((acc * _rms_scale(acc)) * gout_ref[...])
        _for_row_chunks(o_ref.shape[0], body)


def _ffn(x, g_in, g_out, w_in, w_out, layer):
    nf = D_FF // FFN_TF
    vmem = (4 * FFN_TM * D_MODEL * 4
            + FFN_TM * D_MODEL * 2
            + 2 * 3 * D_MODEL * FFN_TF * 2
            + 8 * FFN_TM * FFN_TF * 4)
    return pl.pallas_call(
        _ffn_kernel,
        out_shape=jax.ShapeDtypeStruct((SEQ, D_MODEL), F32),
        grid=(SEQ // FFN_TM, nf),
        in_specs=[
            pl.BlockSpec((FFN_TM, D_MODEL), lambda i, f: (i, 0)),
            pl.BlockSpec((1, D_MODEL), lambda i, f: (0, 0)),
            pl.BlockSpec((1, D_MODEL), lambda i, f: (0, 0)),
            pl.BlockSpec((None, D_MODEL, FFN_TF), lambda i, f: (layer, 0, f)),
            pl.BlockSpec((None, D_MODEL, FFN_TF), lambda i, f: (layer, 0, f + nf)),
            pl.BlockSpec((None, FFN_TF, D_MODEL), lambda i, f: (layer, f, 0)),
        ],
        out_specs=pl.BlockSpec((FFN_TM, D_MODEL), lambda i, f: (i, 0)),
        scratch_shapes=[pltpu.VMEM((FFN_TM, D_MODEL), BF16)],
        compiler_params=pltpu.CompilerParams(
            dimension_semantics=("parallel", "arbitrary"),
            vmem_limit_bytes=_vmem_limit(vmem)),
        name="ffn",
    )(x, g_in, g_out, w_in, w_in, w_out)


PL_TILES = PL_WIDTH // PROJ_TN


def _inproj_kernel(x_ref, g_ref, w_ref, zpl_ref, qkv_ref, qkv16_ref, xn_ref, zs_ref):
    n = pl.program_id(1)

    @pl.when(n == 0)
    def _():
        _norm_to_bf16(x_ref, g_ref, xn_ref)

    @pl.when(n < PL_TILES)
    def _():
        zpl_ref[...] = jnp.dot(xn_ref[...], w_ref[...], preferred_element_type=F32)

    @pl.when(n >= PL_TILES)
    def _():
        z = jnp.dot(xn_ref[...], w_ref[...], preferred_element_type=F32)
        qkv_ref[...] = z.astype(BF16)
        for c in range(PROJ_TN // V7X_LANES):
            zs_ref[c] = z[:, c * V7X_LANES:(c + 1) * V7X_LANES]
        rows_per_plane = PROJ_TM // DIL_PLANES
        for r in range(DIL_PLANES):
            for c in range(PROJ_TN // V7X_LANES):
                plane = zs_ref[c, pl.ds(r, rows_per_plane, stride=DIL_PLANES), :]
                qkv16_ref[r, :, c * V7X_LANES:(c + 1) * V7X_LANES] = plane.astype(BF16)


def _inproj(x, g, w_in, layer):
    rows_per_plane = PROJ_TM // DIL_PLANES
    vmem = (2 * PROJ_TM * D_MODEL * 4 + PROJ_TM * D_MODEL * 2 + 2 * D_MODEL * PROJ_TN * 2
            + 2 * PROJ_TM * PROJ_TN * (4 + 2 + 2) + 4 * PROJ_TM * PROJ_TN * 4)
    qkv_col = lambda n: jnp.maximum(n - PL_TILES, 0)
    return pl.pallas_call(
        _inproj_kernel,
        out_shape=(jax.ShapeDtypeStruct((SEQ, PL_WIDTH), F32),
                   jax.ShapeDtypeStruct((SEQ, 3 * ATTN_WIDTH), BF16),
                   jax.ShapeDtypeStruct((DIL_PLANES, SEQ // DIL_PLANES, 3 * ATTN_WIDTH), BF16)),
        grid=(SEQ // PROJ_TM, IN_WIDTH // PROJ_TN),
        in_specs=[
            pl.BlockSpec((PROJ_TM, D_MODEL), lambda i, n: (i, 0)),
            pl.BlockSpec((1, D_MODEL), lambda i, n: (0, 0)),
            pl.BlockSpec((None, D_MODEL, PROJ_TN), lambda i, n: (layer, 0, n)),
        ],
        out_specs=(
            pl.BlockSpec((PROJ_TM, PROJ_TN), lambda i, n: (i, jnp.minimum(n, PL_TILES - 1))),
            pl.BlockSpec((PROJ_TM, PROJ_TN), lambda i, n: (i, qkv_col(n))),
            pl.BlockSpec((DIL_PLANES, rows_per_plane, PROJ_TN), lambda i, n: (0, i, qkv_col(n))),
        ),
        scratch_shapes=[pltpu.VMEM((PROJ_TM, D_MODEL), BF16),
                        pltpu.VMEM((PROJ_TN // V7X_LANES, PROJ_TM, V7X_LANES), F32)],
        compiler_params=pltpu.CompilerParams(
            dimension_semantics=("parallel", "arbitrary"),
            vmem_limit_bytes=_vmem_limit(vmem)),
        name="mix_inproj",
    )(x, g, w_in)


def _shift_rows(v, k, fill):
    t = v.shape[0]
    if k % 8 == 0:
        pad = jnp.full((k, v.shape[1]), fill, v.dtype)
        return jnp.concatenate([pad, v[: t - k]], axis=0)
    rolled = pltpu.roll(v, k, 0)
    row = lax.broadcasted_iota(jnp.int32, v.shape, 0)
    return jnp.where(row >= k, rolled, fill)


def _mix_kernel(z_ref, pw_ref, ps_ref, cw_ref, cb_ref, wr_ref, wi_ref, br_ref, bi_ref,
                lam_ref, y_ref, ext_ref, h_ref):
    i = pl.program_id(0)
    T = MIX_T
    H = MIX_HALO

    @pl.when(i == 0)
    def _():
        ext_ref[0:H, :] = jnp.zeros((H, POOL_WIDTH + LRU_WIDTH), F32)
        h_ref[...] = jnp.zeros_like(h_ref)

    ext_ref[H:H + T, :] = z_ref[:, 0:POOL_WIDTH + LRU_WIDTH]

    pos = (i * T + 1 + lax.broadcasted_iota(jnp.int32, (T, POOL_GROUP), 0)).astype(F32)
    for g, win in enumerate(POOL_WINDOWS):
        lanes = slice(g * POOL_GROUP, (g + 1) * POOL_GROUP)
        u = ext_ref[H:H + T, lanes]
        s = u
        for k in range(1, win):
            s = s + ext_ref[H - k:H - k + T, lanes]
        pooled = s / jnp.minimum(pos, float(win)) - u
        y = jnp.dot(pooled.astype(BF16), pw_ref[g], preferred_element_type=F32)
        y_ref[:, lanes] = (y * ps_ref[:, lanes]).astype(BF16)

    xl = slice(POOL_WIDTH, POOL_WIDTH + LRU_WIDTH)
    xc = cb_ref[...] + cw_ref[LRU_CONV - 1:LRU_CONV, :] * ext_ref[H:H + T, xl]
    for k in range(1, LRU_CONV):
        xc = xc + cw_ref[LRU_CONV - 1 - k:LRU_CONV - k, :] * ext_ref[H - k:H - k + T, xl]
    xcb = xc.astype(BF16)
    r = jax.nn.sigmoid(jnp.dot(xcb, wr_ref[...], preferred_element_type=F32) + br_ref[...])
    ig = jax.nn.sigmoid(jnp.dot(xcb, wi_ref[...], preferred_element_type=F32) + bi_ref[...])
    nlam = -lam_ref[...]
    softplus = jnp.maximum(nlam, 0.0) + jnp.log1p(jnp.exp(-jnp.abs(nlam)))
    log_a = (-LRU_C * r) * softplus
    a = jnp.exp(log_a)
    b = jnp.sqrt(-jnp.tanh(log_a) * (a * a + 1.0)) * (ig * xc)
    k = 1
    while k < T:
        b = a * _shift_rows(b, k, 0.0) + b
        a = a * _shift_rows(a, k, 1.0)
        k *= 2
    h = a * h_ref[0:1, :] + b
    h_ref[0:1, :] = h[T - 1:T, :]
    gate_in = z_ref[:, POOL_WIDTH + LRU_WIDTH:PL_WIDTH]
    y_ref[:, POOL_WIDTH:POOL_WIDTH + LRU_WIDTH] = (h * jax.nn.gelu(gate_in)).astype(BF16)

    ext_ref[0:H, :] = ext_ref[T:T + H, :]


def _mix(zpl, pool_w, pool_scale, conv_w, conv_b, w_r, w_i, b_r, b_i, lam):
    full = lambda shape: pl.BlockSpec(shape, lambda i: (0,) * len(shape))
    vmem = (2 * MIX_T * PL_WIDTH * 4 + 2 * MIX_T * 1024 * 2 + (MIX_T + MIX_HALO) * 1024 * 4
            + 4 * 512 * 512 * 2 + 24 * MIX_T * LRU_WIDTH * 4)
    return pl.pallas_call(
        _mix_kernel,
        out_shape=jax.ShapeDtypeStruct((SEQ, POOL_WIDTH + LRU_WIDTH), BF16),
        grid=(SEQ // MIX_T,),
        in_specs=[
            pl.BlockSpec((MIX_T, PL_WIDTH), lambda i: (i, 0)),
            full((len(POOL_WINDOWS), POOL_GROUP, POOL_GROUP)),
            full((1, POOL_WIDTH)),
            full((LRU_CONV, LRU_WIDTH)),
            full((1, LRU_WIDTH)),
            full((LRU_WIDTH, LRU_WIDTH)),
            full((LRU_WIDTH, LRU_WIDTH)),
            full((1, LRU_WIDTH)),
            full((1, LRU_WIDTH)),
            full((1, LRU_WIDTH)),
        ],
        out_specs=pl.BlockSpec((MIX_T, POOL_WIDTH + LRU_WIDTH), lambda i: (i, 0)),
        scratch_shapes=[pltpu.VMEM((MIX_T + MIX_HALO, POOL_WIDTH + LRU_WIDTH), F32),
                        pltpu.VMEM((8, LRU_WIDTH), F32)],
        compiler_params=pltpu.CompilerParams(
            dimension_semantics=("arbitrary",),
            vmem_limit_bytes=_vmem_limit(vmem)),
        name="pool_lru",
    )(zpl, pool_w, pool_scale, conv_w, conv_b, w_r, w_i, b_r, b_i, lam)


_CONTRACT_LANES = (((1,), (1,)), ((), ()))
_CONTRACT_ROWS = (((0,), (0,)), ((), ()))


def _mod_pow2(x, n):
    assert n & (n - 1) == 0
    return jnp.bitwise_and(x, n - 1)


def _div_pow2(x, n):
    assert n & (n - 1) == 0
    return jnp.right_shift(x, n.bit_length() - 1)


def _build_bias(hg, bias_s):
    B = ATTN_BLOCK
    u = lax.broadcasted_iota(jnp.int32, (2 * B, 2 * B), 0)
    col = lax.broadcasted_iota(jnp.int32, (2 * B, 2 * B), 1)
    qi = _mod_pow2(col, B)
    heads_per_step = ATTN_W // ATTN_HEAD_DIM
    step_ratio = 2.0 ** (-8.0 * heads_per_step / ATTN_HEADS)
    group_scale = jnp.float32(1.0)
    for grp in range(1, ATTN_WIDTH // ATTN_W):
        group_scale = jnp.where(hg == grp, step_ratio ** grp, group_scale)
    for bi, dil in enumerate(ATTN_DILATIONS):
        if dil == 4:
            delta = (4 * (_mod_pow2(qi, QUARTER) - _mod_pow2(u, 2 * QUARTER) + QUARTER)
                     + _div_pow2(qi, QUARTER) - _div_pow2(u, 2 * QUARTER))
            prev = _mod_pow2(u, 2 * QUARTER) < QUARTER
        else:
            delta = qi - u + B
            prev = u < B
        valid = jnp.logical_and(delta >= 0, delta <= B)
        dist = (delta * dil).astype(F32)
        for hp in range(HEAD_PAIRS):
            slope_a = 2.0 ** (-8.0 * (2 * hp + 1) / ATTN_HEADS)
            slope_b = 2.0 ** (-8.0 * (2 * hp + 2) / ATTN_HEADS)
            slope = group_scale * jnp.where(col < B, slope_a, slope_b)
            bias = (-slope) * dist
            bias_s[bi, 0, hp] = jnp.where(valid, bias, NEG_INF)
            bias_s[bi, 1, hp] = jnp.where(jnp.logical_and(valid, jnp.logical_not(prev)), bias, NEG_INF)


def _attn_unit(q_pair, kcat, vcat, bias):
    B = ATTN_BLOCK
    H = ATTN_HEAD_DIM
    low = lax.broadcasted_iota(jnp.int32, q_pair.shape, 1) < H
    zero = jnp.zeros_like(q_pair)
    qcat = jnp.concatenate([jnp.where(low, q_pair, zero), jnp.where(low, zero, q_pair)], axis=0)
    s = lax.dot_general(kcat, qcat, _CONTRACT_LANES, preferred_element_type=F32) + bias
    m = jnp.max(s, axis=0, keepdims=True)
    p = jnp.exp(s - m)
    l = jnp.sum(p, axis=0, keepdims=True)
    ot = lax.dot_general(vcat, p.astype(BF16), _CONTRACT_ROWS, preferred_element_type=F32)
    inv = 1.0 / l
    lse = m + jnp.log(l)
    o_t = jnp.concatenate([ot[0:H, 0:B] * inv[:, 0:B], ot[H:2 * H, B:2 * B] * inv[:, B:2 * B]], axis=0)
    l_t = jnp.concatenate([jnp.broadcast_to(lse[:, 0:B], (H, B)),
                           jnp.broadcast_to(lse[:, B:2 * B], (H, B))], axis=0)
    return o_t.T, l_t.T


def _attn_kernel(qn, kn, vn, q16, k16, v16, y_ref, kxn, vxn, kx16, vx16, o_s, l_s, bias_s):
    hg = pl.program_id(0)
    j = pl.program_id(1)
    B = ATTN_BLOCK
    T = ATTN_TILE
    first = jnp.where(j == 0, 1, 0)

    @pl.when(j == 0)
    def _():
        kxn[0:B, :] = jnp.zeros((B, ATTN_W), BF16)
        vxn[0:B, :] = jnp.zeros((B, ATTN_W), BF16)
        kx16[:, 0:B, :] = jnp.zeros((DIL_PLANES, B, ATTN_W), BF16)
        vx16[:, 0:B, :] = jnp.zeros((DIL_PLANES, B, ATTN_W), BF16)
        _build_bias(hg, bias_s)

    kxn[B:B + T, :] = kn[...]
    vxn[B:B + T, :] = vn[...]
    kx16[:, B:2 * B, :] = k16[...]
    vx16[:, B:2 * B, :] = v16[...]

    def lanes_of(hp):
        return slice(hp * V7X_LANES, (hp + 1) * V7X_LANES)

    def d1_body(g, carry):
        for i in range(ATTN_GROUP):
            row0 = pl.multiple_of((g * ATTN_GROUP + i) * B, B)
            variant = jnp.where(g == 0, first, 0) if i == 0 else 0
            for hp in range(HEAD_PAIRS):
                o, l = _attn_unit(qn[pl.ds(row0, B), lanes_of(hp)],
                                  kxn[pl.ds(row0, 2 * B), lanes_of(hp)],
                                  vxn[pl.ds(row0, 2 * B), lanes_of(hp)],
                                  bias_s[0, variant, hp])
                o_s[0, hp, pl.ds(row0, B), :] = o
                l_s[0, hp, pl.ds(row0, B), :] = l
        return carry
    lax.fori_loop(0, DIL_PLANES // ATTN_GROUP, d1_body, 0)

    def d4_body(c, carry):
        for b in range(ATTN_BLOCK // QUARTER):
            variant = first if b == 0 else 0
            q_rows = slice(b * QUARTER, (b + 1) * QUARTER)
            k_rows = slice(B + (b - 1) * QUARTER, B + (b + 1) * QUARTER)
            for hp in range(HEAD_PAIRS):
                q_pair = jnp.concatenate([q16[4 * a + c, q_rows, lanes_of(hp)] for a in range(4)], axis=0)
                kcat = jnp.concatenate([kx16[4 * a + c, k_rows, lanes_of(hp)] for a in range(4)], axis=0)
                vcat = jnp.concatenate([vx16[4 * a + c, k_rows, lanes_of(hp)] for a in range(4)], axis=0)
                o, l = _attn_unit(q_pair, kcat, vcat, bias_s[1, variant, hp])
                for a in range(4):
                    rows = pl.ds(b * QUARTER * DIL_PLANES + 4 * a + c, QUARTER, stride=DIL_PLANES)
                    o_s[1, hp, rows, :] = o[a * QUARTER:(a + 1) * QUARTER]
                    l_s[1, hp, rows, :] = l[a * QUARTER:(a + 1) * QUARTER]
        return carry
    lax.fori_loop(0, 4, d4_body, 0)

    def d16_body(g, carry):
        for i in range(ATTN_GROUP):
            r = g * ATTN_GROUP + i
            for hp in range(HEAD_PAIRS):
                o, l = _attn_unit(q16[r, :, lanes_of(hp)], kx16[r, :, lanes_of(hp)],
                                  vx16[r, :, lanes_of(hp)], bias_s[2, first, hp])
                rows = pl.ds(r, B, stride=DIL_PLANES)
                o_s[2, hp, rows, :] = o
                l_s[2, hp, rows, :] = l
        return carry
    lax.fori_loop(0, DIL_PLANES // ATTN_GROUP, d16_body, 0)

    def merge(c, carry):
        rows = pl.ds(pl.multiple_of(c * COMBINE_ROWS, COMBINE_ROWS), COMBINE_ROWS)
        for hp in range(HEAD_PAIRS):
            l1 = l_s[0, hp, rows, :]
            l4 = l_s[1, hp, rows, :]
            l16 = l_s[2, hp, rows, :]
            top = jnp.maximum(jnp.maximum(l1, l4), l16)
            w1 = jnp.exp(l1 - top)
            w4 = jnp.exp(l4 - top)
            w16 = jnp.exp(l16 - top)
            num = w1 * o_s[0, hp, rows, :] + w4 * o_s[1, hp, rows, :] + w16 * o_s[2, hp, rows, :]
            y_ref[rows, lanes_of(hp)] = (num / (w1 + w4 + w16)).astype(BF16)
        return carry
    lax.fori_loop(0, T // COMBINE_ROWS, merge, 0)

    kxn[0:B, :] = kxn[T:T + B, :]
    vxn[0:B, :] = vxn[T:T + B, :]
    kx16[:, 0:B, :] = kx16[:, B:2 * B, :]
    vx16[:, 0:B, :] = vx16[:, B:2 * B, :]


def _attention(qkv, qkv16):
    B = ATTN_BLOCK
    ngrp = ATTN_WIDTH // ATTN_W
    nat = lambda which: pl.BlockSpec((ATTN_TILE, ATTN_W), lambda hg, j: (j, which * ngrp + hg))
    planes = lambda which: pl.BlockSpec((DIL_PLANES, B, ATTN_W), lambda hg, j: (0, j, which * ngrp + hg))
    vmem = (2 * 6 * ATTN_TILE * ATTN_W * 2 + 2 * ATTN_TILE * ATTN_W * 2
            + 2 * (ATTN_TILE + B) * ATTN_W * 2 + 2 * DIL_PLANES * 2 * B * ATTN_W * 2
            + 2 * 3 * HEAD_PAIRS * ATTN_TILE * V7X_LANES * 4
            + 3 * 2 * HEAD_PAIRS * 4 * B * B * 4
            + 16 * 1024 * 1024)
    return pl.pallas_call(
        _attn_kernel,
        out_shape=jax.ShapeDtypeStruct((SEQ, ATTN_WIDTH), BF16),
        grid=(ngrp, SEQ // ATTN_TILE),
        in_specs=[nat(0), nat(1), nat(2), planes(0), planes(1), planes(2)],
        out_specs=pl.BlockSpec((ATTN_TILE, ATTN_W), lambda hg, j: (j, hg)),
        scratch_shapes=[
            pltpu.VMEM((ATTN_TILE + B, ATTN_W), BF16),
            pltpu.VMEM((ATTN_TILE + B, ATTN_W), BF16),
            pltpu.VMEM((DIL_PLANES, 2 * B, ATTN_W), BF16),
            pltpu.VMEM((DIL_PLANES, 2 * B, ATTN_W), BF16),
            pltpu.VMEM((3, HEAD_PAIRS, ATTN_TILE, V7X_LANES), F32),
            pltpu.VMEM((3, HEAD_PAIRS, ATTN_TILE, V7X_LANES), F32),
            pltpu.VMEM((3, 2, HEAD_PAIRS, 2 * B, 2 * B), F32),
        ],
        compiler_params=pltpu.CompilerParams(
            dimension_semantics=("arbitrary", "arbitrary"),
            vmem_limit_bytes=_vmem_limit(vmem)),
        name="dilated_attn",
    )(qkv, qkv, qkv, qkv16, qkv16, qkv16)


def _outproj_kernel(x_ref, ypl_ref, ya_ref, g_ref, w_ref, o_ref):
    half = POOL_WIDTH + LRU_WIDTH

    def body(rows):
        h = (jnp.dot(ypl_ref[rows, :], w_ref[0:half, :], preferred_element_type=F32)
             + jnp.dot(ya_ref[rows, :], w_ref[half:2 * half, :], preferred_element_type=F32))
        o_ref[rows, :] = x_ref[rows, :] + (h * _rms_scale(h)) * g_ref[...]
    _for_row_chunks(o_ref.shape[0], body)


def _outproj(x, ypl, ya, g, w_out, layer):
    half = POOL_WIDTH + LRU_WIDTH
    vmem = (4 * OUT_TM * D_MODEL * 4 + 4 * OUT_TM * half * 2 + 2 * D_MODEL * D_MODEL * 2
            + 4 * OUT_TM * D_MODEL * 4)
    return pl.pallas_call(
        _outproj_kernel,
        out_shape=jax.ShapeDtypeStruct((SEQ, D_MODEL), F32),
        grid=(SEQ // OUT_TM,),
        in_specs=[
            pl.BlockSpec((OUT_TM, D_MODEL), lambda i: (i, 0)),
            pl.BlockSpec((OUT_TM, half), lambda i: (i, 0)),
            pl.BlockSpec((OUT_TM, ATTN_WIDTH), lambda i: (i, 0)),
            pl.BlockSpec((1, D_MODEL), lambda i: (0, 0)),
            pl.BlockSpec((None, D_MODEL, D_MODEL), lambda i: (layer, 0, 0)),
        ],
        out_specs=pl.BlockSpec((OUT_TM, D_MODEL), lambda i: (i, 0)),
        compiler_params=pltpu.CompilerParams(
            dimension_semantics=("parallel",),
            vmem_limit_bytes=_vmem_limit(vmem)),
        name="mix_outproj",
    )(x, ypl, ya, g, w_out)


def _block_diag(w):
    heads, c, _ = w.shape
    eye = jnp.eye(heads, dtype=w.dtype)
    return (eye[:, None, :, None] * w[:, :, None, :]).reshape(heads * c, heads * c)


def kernel(x, norm_g, ffn1_w_in, ffn1_w_out, mix_w_in, mix_w_out, pool_w, pool_scale,
           lru_conv_w, lru_conv_b, lru_gate_w, lru_gate_b, lru_lambda, ffn2_w_in, ffn2_w_out):
    xs = x.reshape(SEQ, D_MODEL)
    w1_in = ffn1_w_in.astype(BF16)
    w1_out = ffn1_w_out.astype(BF16)
    w2_in = ffn2_w_in.astype(BF16)
    w2_out = ffn2_w_out.astype(BF16)
    qscale = jnp.concatenate([jnp.ones((PL_WIDTH,), F32),
                              jnp.full((ATTN_WIDTH,), ATTN_HEAD_DIM ** -0.5, F32),
                              jnp.ones((2 * ATTN_WIDTH,), F32)])
    wm_in = (mix_w_in * qscale).astype(BF16)
    wm_out = mix_w_out.astype(BF16)
    pw = pool_w.astype(BF16)
    for layer in range(DEPTH):
        g = norm_g[layer].reshape(6, 1, D_MODEL)
        xs = _ffn(xs, g[0], g[1], w1_in, w1_out, layer)
        zpl, qkv, qkv16 = _inproj(xs, g[2], wm_in, layer)
        ypl = _mix(zpl, pw[layer], pool_scale[layer].reshape(1, POOL_WIDTH),
                   lru_conv_w[layer], lru_conv_b[layer].reshape(1, LRU_WIDTH),
                   _block_diag(lru_gate_w[layer, 0]).astype(BF16),
                   _block_diag(lru_gate_w[layer, 1]).astype(BF16),
                   lru_gate_b[layer, 0].reshape(1, LRU_WIDTH),
                   lru_gate_b[layer, 1].reshape(1, LRU_WIDTH),
                   lru_lambda[layer].reshape(1, LRU_WIDTH))
        ya = _attention(qkv, qkv16)
        xs = _outproj(xs, ypl, ya, g[3], wm_out, layer)
        xs = _ffn(xs, g[4], g[5], w2_in, w2_out, layer)
    return xs.reshape(x.shape)
```

```python
import jax
import jax.numpy as jnp
from jax import lax
from jax.experimental import pallas as pl
from jax.experimental.pallas import tpu as pltpu

F32 = jnp.float32
BF16 = jnp.bfloat16

D_MODEL = 2048
SEQ = 16384
DEPTH = 2
D_FF = 5632
POOL_WIDTH = 512
POOL_WINDOWS = (2, 4, 8, 16)
POOL_GROUP = 128
LRU_WIDTH = 512
LRU_HEADS = 8
LRU_HEAD_DIM = 64
LRU_CONV = 4
LRU_C = 8.0
ATTN_WIDTH = 1024
ATTN_HEAD_DIM = 64
ATTN_HEADS = 16
ATTN_DILATIONS = (1, 4, 16)
ATTN_BLOCK = 128
PL_WIDTH = POOL_WIDTH + 2 * LRU_WIDTH
IN_WIDTH = PL_WIDTH + 3 * ATTN_WIDTH
NORM_EPS = 1e-6
NEG_INF = -1e30

V7X_LANES = 128
VMEM_LIMIT_CAP = 60000 * 1024

FFN_TM = 1024
FFN_TF = 512
PROJ_TM = 1024
PROJ_TN = 512
OUT_TM = 512
OUT_ROWS = 256
MIX_T = 512
MIX_HALO = 16
DIL_PLANES = 16
DIL_STEP = 4
ATTN_TILE = DIL_PLANES * ATTN_BLOCK
ATTN_W = 256
HEAD_PAIRS = ATTN_W // V7X_LANES
ATTN_GROUP = 4
QUARTER = ATTN_BLOCK // 4
COMBINE_ROWS = 256
NORM_ROWS = 128
CAST_BLOCK_BYTES = 8 * 1024 * 1024


def _vmem_limit(nbytes):
    return int(min(VMEM_LIMIT_CAP, nbytes))


def _rms_scale(x):
    return lax.rsqrt(jnp.mean(x * x, axis=-1, keepdims=True) + NORM_EPS)


def _for_row_chunks(n_rows, body, chunk=NORM_ROWS):
    def step(c, carry):
        body(pl.ds(pl.multiple_of(c * chunk, chunk), chunk))
        return carry
    lax.fori_loop(0, n_rows // chunk, step, 0)


def _norm_to_bf16(x_ref, g_ref, xn_ref):
    def body(rows):
        x = x_ref[rows, :]
        xn_ref[rows, :] = ((x * _rms_scale(x)) * g_ref[...]).astype(BF16)
    _for_row_chunks(x_ref.shape[0], body)


def _ffn_kernel(x_ref, gin_ref, gout_ref, wg_ref, wu_ref, wo_ref, o_ref, xn_ref):
    f = pl.program_id(1)

    @pl.when(f == 0)
    def _():
        _norm_to_bf16(x_ref, gin_ref, xn_ref)
        o_ref[...] = jnp.zeros_like(o_ref)

    xn = xn_ref[...]
    gate = jnp.dot(xn, wg_ref[...], preferred_element_type=F32)
    up = jnp.dot(xn, wu_ref[...], preferred_element_type=F32)
    h = ((gate * jax.nn.sigmoid(gate)) * up).astype(BF16)
    o_ref[...] += jnp.dot(h, wo_ref[...], preferred_element_type=F32)

    @pl.when(f == pl.num_programs(1) - 1)
    def _():
        def body(rows):
            acc = o_ref[rows, :]
            o_ref[rows, :] = x_ref[rows, :] + 0.5 * ((acc * _rms_scale(acc)) * gout_ref[...])
        _for_row_chunks(o_ref.shape[0], body)


def _ffn(x, g_in, g_out, w_in, w_out, layer):
    nf = D_FF // FFN_TF
    vmem = (4 * FFN_TM * D_MODEL * 4
            + FFN_TM * D_MODEL * 2
            + 2 * 3 * D_MODEL * FFN_TF * 2
            + 8 * FFN_TM * FFN_TF * 4)
    return pl.pallas_call(
        _ffn_kernel,
        out_shape=jax.ShapeDtypeStruct((SEQ, D_MODEL), F32),
        grid=(SEQ // FFN_TM, nf),
        in_specs=[
            pl.BlockSpec((FFN_TM, D_MODEL), lambda i, f: (i, 0)),
            pl.BlockSpec((1, D_MODEL), lambda i, f: (0, 0)),
            pl.BlockSpec((1, D_MODEL), lambda i, f: (0, 0)),
            pl.BlockSpec((None, D_MODEL, FFN_TF), lambda i, f: (layer, 0, f)),
            pl.BlockSpec((None, D_MODEL, FFN_TF), lambda i, f: (layer, 0, f + nf)),
            pl.BlockSpec((None, FFN_TF, D_MODEL), lambda i, f: (layer, f, 0)),
        ],
        out_specs=pl.BlockSpec((FFN_TM, D_MODEL), lambda i, f: (i, 0)),
        scratch_shapes=[pltpu.VMEM((FFN_TM, D_MODEL), BF16)],
        compiler_params=pltpu.CompilerParams(
            dimension_semantics=("parallel", "arbitrary"),
            vmem_limit_bytes=_vmem_limit(vmem)),
        name="ffn",
    )(x, g_in, g_out, w_in, w_in, w_out)


PL_TILES = PL_WIDTH // PROJ_TN


def _inproj_kernel(x_ref, g_ref, w_ref, zpl_ref, qkv_ref, qkv16_ref, xn_ref, zs_ref, zt_ref):
    n = pl.program_id(1)

    @pl.when(n == 0)
    def _():
        _norm_to_bf16(x_ref, g_ref, xn_ref)

    @pl.when(n < PL_TILES)
    def _():
        zpl_ref[...] = jnp.dot(xn_ref[...], w_ref[...], preferred_element_type=F32)

    @pl.when(n >= PL_TILES)
    def _():
        z = jnp.dot(xn_ref[...], w_ref[...], preferred_element_type=F32)
        qkv_ref[...] = z.astype(BF16)
        for c in range(PROJ_TN // V7X_LANES):
            lanes = slice(c * V7X_LANES, (c + 1) * V7X_LANES)
            zs_ref[c] = z[:, lanes]
            for r1 in range(DIL_STEP):
                zt_ref[c, r1] = zs_ref[c, pl.ds(r1, PROJ_TM // DIL_STEP, stride=DIL_STEP), :]
            for r1 in range(DIL_STEP):
                for r2 in range(DIL_STEP):
                    plane = zt_ref[c, r1, pl.ds(r2, PROJ_TM // DIL_PLANES, stride=DIL_STEP), :]
                    qkv16_ref[r1 + DIL_STEP * r2, :, lanes] = plane.astype(BF16)


def _inproj(x, g, w_in, layer):
    rows_per_plane = PROJ_TM // DIL_PLANES
    vmem = (2 * PROJ_TM * D_MODEL * 4 + PROJ_TM * D_MODEL * 2 + 2 * D_MODEL * PROJ_TN * 2
            + 2 * PROJ_TM * PROJ_TN * (4 + 2 + 2) + 4 * PROJ_TM * PROJ_TN * 4)
    qkv_col = lambda n: jnp.maximum(n - PL_TILES, 0)
    return pl.pallas_call(
        _inproj_kernel,
        out_shape=(jax.ShapeDtypeStruct((SEQ, PL_WIDTH), F32),
                   jax.ShapeDtypeStruct((SEQ, 3 * ATTN_WIDTH), BF16),
                   jax.ShapeDtypeStruct((DIL_PLANES, SEQ // DIL_PLANES, 3 * ATTN_WIDTH), BF16)),
        grid=(SEQ // PROJ_TM, IN_WIDTH // PROJ_TN),
        in_specs=[
            pl.BlockSpec((PROJ_TM, D_MODEL), lambda i, n: (i, 0)),
            pl.BlockSpec((1, D_MODEL), lambda i, n: (0, 0)),
            pl.BlockSpec((None, D_MODEL, PROJ_TN), lambda i, n: (layer, 0, n)),
        ],
        out_specs=(
            pl.BlockSpec((PROJ_TM, PROJ_TN), lambda i, n: (i, jnp.minimum(n, PL_TILES - 1))),
            pl.BlockSpec((PROJ_TM, PROJ_TN), lambda i, n: (i, qkv_col(n))),
            pl.BlockSpec((DIL_PLANES, rows_per_plane, PROJ_TN), lambda i, n: (0, i, qkv_col(n))),
        ),
        scratch_shapes=[pltpu.VMEM((PROJ_TM, D_MODEL), BF16),
                        pltpu.VMEM((PROJ_TN // V7X_LANES, PROJ_TM, V7X_LANES), F32),
                        pltpu.VMEM((PROJ_TN // V7X_LANES, DIL_STEP, PROJ_TM // DIL_STEP, V7X_LANES), F32)],
        compiler_params=pltpu.CompilerParams(
            dimension_semantics=("parallel", "arbitrary"),
            vmem_limit_bytes=_vmem_limit(vmem)),
        name="mix_inproj",
    )(x, g, w_in)


def _shift_rows(v, k, fill):
    t = v.shape[0]
    if k % 8 == 0:
        pad = jnp.full((k, v.shape[1]), fill, v.dtype)
        return jnp.concatenate([pad, v[: t - k]], axis=0)
    rolled = pltpu.roll(v, k, 0)
    row = lax.broadcasted_iota(jnp.int32, v.shape, 0)
    return jnp.where(row >= k, rolled, fill)


def _mix_kernel(z_ref, pw_ref, ps_ref, cw_ref, cb_ref, wr_ref, wi_ref, br_ref, bi_ref,
                lam_ref, y_ref, ext_ref, h_ref):
    i = pl.program_id(0)
    T = MIX_T
    H = MIX_HALO

    @pl.when(i == 0)
    def _():
        ext_ref[0:H, :] = jnp.zeros((H, POOL_WIDTH + LRU_WIDTH), F32)
        h_ref[...] = jnp.zeros_like(h_ref)

    ext_ref[H:H + T, :] = z_ref[:, 0:POOL_WIDTH + LRU_WIDTH]

    pos = (i * T + 1 + lax.broadcasted_iota(jnp.int32, (T, POOL_GROUP), 0)).astype(F32)
    for g, win in enumerate(POOL_WINDOWS):
        lanes = slice(g * POOL_GROUP, (g + 1) * POOL_GROUP)
        u = ext_ref[H:H + T, lanes]
        s = u
        for k in range(1, win):
            s = s + ext_ref[H - k:H - k + T, lanes]
        pooled = s / jnp.minimum(pos, float(win)) - u
        y = jnp.dot(pooled.astype(BF16), pw_ref[g], preferred_element_type=F32)
        y_ref[:, lanes] = (y * ps_ref[:, lanes]).astype(BF16)

    xl = slice(POOL_WIDTH, POOL_WIDTH + LRU_WIDTH)
    xc = cb_ref[...] + cw_ref[LRU_CONV - 1:LRU_CONV, :] * ext_ref[H:H + T, xl]
    for k in range(1, LRU_CONV):
        xc = xc + cw_ref[LRU_CONV - 1 - k:LRU_CONV - k, :] * ext_ref[H - k:H - k + T, xl]
    xcb = xc.astype(BF16)
    r = jax.nn.sigmoid(jnp.dot(xcb, wr_ref[...], preferred_element_type=F32) + br_ref[...])
    ig = jax.nn.sigmoid(jnp.dot(xcb, wi_ref[...], preferred_element_type=F32) + bi_ref[...])
    nlam = -lam_ref[...]
    softplus = jnp.maximum(nlam, 0.0) + jnp.log1p(jnp.exp(-jnp.abs(nlam)))
    log_a = (-LRU_C * r) * softplus
    a = jnp.exp(log_a)
    b = jnp.sqrt(-jnp.tanh(log_a) * (a * a + 1.0)) * (ig * xc)
    k = 1
    while k < T:
        b = a * _shift_rows(b, k, 0.0) + b
        a = a * _shift_rows(a, k, 1.0)
        k *= 2
    h = a * h_ref[0:1, :] + b
    h_ref[0:1, :] = h[T - 1:T, :]
    gate_in = z_ref[:, POOL_WIDTH + LRU_WIDTH:PL_WIDTH]
    y_ref[:, POOL_WIDTH:POOL_WIDTH + LRU_WIDTH] = (h * jax.nn.gelu(gate_in)).astype(BF16)

    ext_ref[0:H, :] = ext_ref[T:T + H, :]


def _mix(zpl, pool_w, pool_scale, conv_w, conv_b, w_r, w_i, b_r, b_i, lam):
    full = lambda shape: pl.BlockSpec(shape, lambda i: (0,) * len(shape))
    vmem = (2 * MIX_T * PL_WIDTH * 4 + 2 * MIX_T * 1024 * 2 + (MIX_T + MIX_HALO) * 1024 * 4
            + 4 * 512 * 512 * 2 + 24 * MIX_T * LRU_WIDTH * 4)
    return pl.pallas_call(
        _mix_kernel,
        out_shape=jax.ShapeDtypeStruct((SEQ, POOL_WIDTH + LRU_WIDTH), BF16),
        grid=(SEQ // MIX_T,),
        in_specs=[
            pl.BlockSpec((MIX_T, PL_WIDTH), lambda i: (i, 0)),
            full((len(POOL_WINDOWS), POOL_GROUP, POOL_GROUP)),
            full((1, POOL_WIDTH)),
            full((LRU_CONV, LRU_WIDTH)),
            full((1, LRU_WIDTH)),
            full((LRU_WIDTH, LRU_WIDTH)),
            full((LRU_WIDTH, LRU_WIDTH)),
            full((1, LRU_WIDTH)),
            full((1, LRU_WIDTH)),
            full((1, LRU_WIDTH)),
        ],
        out_specs=pl.BlockSpec((MIX_T, POOL_WIDTH + LRU_WIDTH), lambda i: (i, 0)),
        scratch_shapes=[pltpu.VMEM((MIX_T + MIX_HALO, POOL_WIDTH + LRU_WIDTH), F32),
                        pltpu.VMEM((8, LRU_WIDTH), F32)],
        compiler_params=pltpu.CompilerParams(
            dimension_semantics=("arbitrary",),
            vmem_limit_bytes=_vmem_limit(vmem)),
        name="pool_lru",
    )(zpl, pool_w, pool_scale, conv_w, conv_b, w_r, w_i, b_r, b_i, lam)


_CONTRACT_LANES = (((1,), (1,)), ((), ()))
_CONTRACT_ROWS = (((0,), (0,)), ((), ()))


def _mod_pow2(x, n):
    assert n & (n - 1) == 0
    return jnp.bitwise_and(x, n - 1)


def _div_pow2(x, n):
    assert n & (n - 1) == 0
    return jnp.right_shift(x, n.bit_length() - 1)


def _build_bias(hg, bias_s):
    B = ATTN_BLOCK
    u = lax.broadcasted_iota(jnp.int32, (2 * B, 2 * B), 0)
    col = lax.broadcasted_iota(jnp.int32, (2 * B, 2 * B), 1)
    qi = _mod_pow2(col, B)
    heads_per_step = ATTN_W // ATTN_HEAD_DIM
    step_ratio = 2.0 ** (-8.0 * heads_per_step / ATTN_HEADS)
    group_scale = jnp.float32(1.0)
    for grp in range(1, ATTN_WIDTH // ATTN_W):
        group_scale = jnp.where(hg == grp, step_ratio ** grp, group_scale)
    for bi, dil in enumerate(ATTN_DILATIONS):
        if dil == 4:
            delta = (4 * (_mod_pow2(qi, QUARTER) - _mod_pow2(u, 2 * QUARTER) + QUARTER)
                     + _div_pow2(qi, QUARTER) - _div_pow2(u, 2 * QUARTER))
            prev = _mod_pow2(u, 2 * QUARTER) < QUARTER
        else:
            delta = qi - u + B
            prev = u < B
        valid = jnp.logical_and(delta >= 0, delta <= B)
        dist = (delta * dil).astype(F32)
        for hp in range(HEAD_PAIRS):
            slope_a = 2.0 ** (-8.0 * (2 * hp + 1) / ATTN_HEADS)
            slope_b = 2.0 ** (-8.0 * (2 * hp + 2) / ATTN_HEADS)
            slope = group_scale * jnp.where(col < B, slope_a, slope_b)
            bias = (-slope) * dist
            bias_s[bi, 0, hp] = jnp.where(valid, bias, NEG_INF)
            bias_s[bi, 1, hp] = jnp.where(jnp.logical_and(valid, jnp.logical_not(prev)), bias, NEG_INF)


def _scores_stage(q_pair, kcat, bias, s_ref, m_ref, slot):
    low = lax.broadcasted_iota(jnp.int32, q_pair.shape, 1) < ATTN_HEAD_DIM
    zero = jnp.zeros_like(q_pair)
    qcat = jnp.concatenate([jnp.where(low, q_pair, zero), jnp.where(low, zero, q_pair)], axis=0)
    s = lax.dot_general(kcat, qcat, _CONTRACT_LANES, preferred_element_type=F32) + bias
    s_ref[slot] = s
    m_ref[slot, 0:1, :] = jnp.max(s, axis=0, keepdims=True)


def _probs_stage(vcat, s_ref, m_ref, ot_ref, ml_ref, slot):
    m = m_ref[slot, 0:1, :]
    p = jnp.exp(s_ref[slot] - m)
    ot_ref[slot] = lax.dot_general(vcat, p.astype(BF16), _CONTRACT_ROWS, preferred_element_type=F32)
    ml_ref[slot, 0:1, :] = m
    ml_ref[slot, 1:2, :] = jnp.sum(p, axis=0, keepdims=True)


def _finish_stage(ot_ref, ml_ref, slot):
    B = ATTN_BLOCK
    H = ATTN_HEAD_DIM
    ot = ot_ref[slot]
    m = ml_ref[slot, 0:1, :]
    l = ml_ref[slot, 1:2, :]
    inv = 1.0 / l
    lse = m + jnp.log(l)
    o_t = jnp.concatenate([ot[0:H, 0:B] * inv[:, 0:B], ot[H:2 * H, B:2 * B] * inv[:, B:2 * B]], axis=0)
    l_t = jnp.concatenate([jnp.broadcast_to(lse[:, 0:B], (H, B)),
                           jnp.broadcast_to(lse[:, B:2 * B], (H, B))], axis=0)
    return o_t.T, l_t.T


def _software_pipeline(n, scores, probs, finish):
    scores(0)
    probs(0)
    scores(1)

    def body(t, carry):
        finish(t - 2)
        probs(t - 1)
        scores(t)
        return carry
    lax.fori_loop(2, n, body, 0)
    finish(n - 2)
    probs(n - 1)
    finish(n - 1)


def _attn_kernel(qn, kn, vn, q16, k16, v16, y_ref, kxn, vxn, kx16, vx16, o_s, l_s, bias_s, *stage):
    hg = pl.program_id(0)
    j = pl.program_id(1)
    B = ATTN_BLOCK
    T = ATTN_TILE
    first = jnp.where(j == 0, 1, 0)

    @pl.when(j == 0)
    def _():
        kxn[0:B, :] = jnp.zeros((B, ATTN_W), BF16)
        vxn[0:B, :] = jnp.zeros((B, ATTN_W), BF16)
        kx16[:, 0:B, :] = jnp.zeros((DIL_PLANES, B, ATTN_W), BF16)
        vx16[:, 0:B, :] = jnp.zeros((DIL_PLANES, B, ATTN_W), BF16)
        _build_bias(hg, bias_s)

    kxn[B:B + T, :] = kn[...]
    vxn[B:B + T, :] = vn[...]
    kx16[:, B:2 * B, :] = k16[...]
    vx16[:, B:2 * B, :] = v16[...]

    def lanes_of(hp):
        return slice(hp * V7X_LANES, (hp + 1) * V7X_LANES)

    def aligned(x, m):
        return x if isinstance(x, int) else pl.multiple_of(x, m)

    def run_branch(bi, fetch_q, fetch_k, fetch_v, variant_of, scatter):
        def units(t):
            for i in range(ATTN_GROUP):
                for hp in range(HEAD_PAIRS):
                    yield t * ATTN_GROUP + i, hp, stage[4 * (i * HEAD_PAIRS + hp):4 * (i * HEAD_PAIRS + hp) + 4]

        def scores(t):
            for blk, hp, (s_ref, m_ref, _, _) in units(t):
                _scores_stage(fetch_q(blk, hp), fetch_k(blk, hp), bias_s[bi, variant_of(blk), hp],
                              s_ref, m_ref, t % 2)

        def probs(t):
            for blk, hp, (s_ref, m_ref, ot_ref, ml_ref) in units(t):
                _probs_stage(fetch_v(blk, hp), s_ref, m_ref, ot_ref, ml_ref, t % 2)

        def finish(t):
            for blk, hp, (_, _, ot_ref, ml_ref) in units(t):
                o, l = _finish_stage(ot_ref, ml_ref, t % 2)
                scatter(blk, hp, o, l)

        _software_pipeline(DIL_PLANES // ATTN_GROUP, scores, probs, finish)

    def d1_rows(blk, n):
        return pl.ds(aligned(blk * B, B), n)

    def d1_scatter(blk, hp, o, l):
        o_s[0, hp, d1_rows(blk, B), :] = o
        l_s[0, hp, d1_rows(blk, B), :] = l

    run_branch(0,
               lambda blk, hp: qn[d1_rows(blk, B), lanes_of(hp)],
               lambda blk, hp: kxn[d1_rows(blk, 2 * B), lanes_of(hp)],
               lambda blk, hp: vxn[d1_rows(blk, 2 * B), lanes_of(hp)],
               lambda blk: jnp.where(blk == 0, first, 0),
               d1_scatter)

    per_plane = ATTN_BLOCK // QUARTER

    def d4_gather(ref, blk, hp, row0, n):
        c = blk // per_plane
        rows = pl.ds(aligned(row0 + (blk % per_plane) * QUARTER, QUARTER), n)
        return jnp.concatenate([ref[per_plane * a + c, rows, lanes_of(hp)] for a in range(per_plane)], axis=0)

    def d4_scatter(blk, hp, o, l):
        c = blk // per_plane
        b = blk % per_plane
        for a in range(per_plane):
            rows = pl.ds(b * QUARTER * DIL_PLANES + per_plane * a + c, QUARTER, stride=DIL_PLANES)
            o_s[1, hp, rows, :] = o[a * QUARTER:(a + 1) * QUARTER]
            l_s[1, hp, rows, :] = l[a * QUARTER:(a + 1) * QUARTER]

    run_branch(1,
               lambda blk, hp: d4_gather(q16, blk, hp, 0, QUARTER),
               lambda blk, hp: d4_gather(kx16, blk, hp, B - QUARTER, 2 * QUARTER),
               lambda blk, hp: d4_gather(vx16, blk, hp, B - QUARTER, 2 * QUARTER),
               lambda blk: jnp.where(blk % per_plane == 0, first, 0),
               d4_scatter)

    def d16_scatter(blk, hp, o, l):
        rows = pl.ds(blk, B, stride=DIL_PLANES)
        o_s[2, hp, rows, :] = o
        l_s[2, hp, rows, :] = l

    run_branch(2,
               lambda blk, hp: q16[blk, :, lanes_of(hp)],
               lambda blk, hp: kx16[blk, :, lanes_of(hp)],
               lambda blk, hp: vx16[blk, :, lanes_of(hp)],
               lambda blk: first,
               d16_scatter)

    def merge(c, carry):
        rows = pl.ds(pl.multiple_of(c * COMBINE_ROWS, COMBINE_ROWS), COMBINE_ROWS)
        for hp in range(HEAD_PAIRS):
            l1 = l_s[0, hp, rows, :]
            l4 = l_s[1, hp, rows, :]
            l16 = l_s[2, hp, rows, :]
            top = jnp.maximum(jnp.maximum(l1, l4), l16)
            w1 = jnp.exp(l1 - top)
            w4 = jnp.exp(l4 - top)
            w16 = jnp.exp(l16 - top)
            num = w1 * o_s[0, hp, rows, :] + w4 * o_s[1, hp, rows, :] + w16 * o_s[2, hp, rows, :]
            y_ref[rows, lanes_of(hp)] = (num / (w1 + w4 + w16)).astype(BF16)
        return carry
    lax.fori_loop(0, T // COMBINE_ROWS, merge, 0)

    kxn[0:B, :] = kxn[T:T + B, :]
    vxn[0:B, :] = vxn[T:T + B, :]
    kx16[:, 0:B, :] = kx16[:, B:2 * B, :]
    vx16[:, 0:B, :] = vx16[:, B:2 * B, :]


def _attention(qkv, qkv16):
    B = ATTN_BLOCK
    ngrp = ATTN_WIDTH // ATTN_W
    nat = lambda which: pl.BlockSpec((ATTN_TILE, ATTN_W), lambda hg, j: (j, which * ngrp + hg))
    planes = lambda which: pl.BlockSpec((DIL_PLANES, B, ATTN_W), lambda hg, j: (0, j, which * ngrp + hg))
    vmem = (2 * 6 * ATTN_TILE * ATTN_W * 2 + 2 * ATTN_TILE * ATTN_W * 2
            + 2 * (ATTN_TILE + B) * ATTN_W * 2 + 2 * DIL_PLANES * 2 * B * ATTN_W * 2
            + 2 * 3 * HEAD_PAIRS * ATTN_TILE * V7X_LANES * 4
            + 3 * 2 * HEAD_PAIRS * 4 * B * B * 4
            + 16 * 1024 * 1024)
    return pl.pallas_call(
        _attn_kernel,
        out_shape=jax.ShapeDtypeStruct((SEQ, ATTN_WIDTH), BF16),
        grid=(ngrp, SEQ // ATTN_TILE),
        in_specs=[nat(0), nat(1), nat(2), planes(0), planes(1), planes(2)],
        out_specs=pl.BlockSpec((ATTN_TILE, ATTN_W), lambda hg, j: (j, hg)),
        scratch_shapes=[
            pltpu.VMEM((ATTN_TILE + B, ATTN_W), BF16),
            pltpu.VMEM((ATTN_TILE + B, ATTN_W), BF16),
            pltpu.VMEM((DIL_PLANES, 2 * B, ATTN_W), BF16),
            pltpu.VMEM((DIL_PLANES, 2 * B, ATTN_W), BF16),
            pltpu.VMEM((3, HEAD_PAIRS, ATTN_TILE, V7X_LANES), F32),
            pltpu.VMEM((3, HEAD_PAIRS, ATTN_TILE, V7X_LANES), F32),
            pltpu.VMEM((3, 2, HEAD_PAIRS, 2 * B, 2 * B), F32),
        ] + [
            pltpu.VMEM(shape, F32)
            for _ in range(ATTN_GROUP * HEAD_PAIRS)
            for shape in ((2, 2 * B, 2 * B), (2, 8, 2 * B), (2, B, 2 * B), (2, 8, 2 * B))
        ],
        compiler_params=pltpu.CompilerParams(
            dimension_semantics=("arbitrary", "arbitrary"),
            vmem_limit_bytes=_vmem_limit(vmem)),
        name="dilated_attn",
    )(qkv, qkv, qkv, qkv16, qkv16, qkv16)


def _outproj_kernel(x_ref, ypl_ref, ya_ref, g_ref, w_ref, o_ref):
    half = POOL_WIDTH + LRU_WIDTH

    def body(rows):
        h = (jnp.dot(ypl_ref[rows, :], w_ref[0:half, :], preferred_element_type=F32)
             + jnp.dot(ya_ref[rows, :], w_ref[half:2 * half, :], preferred_element_type=F32))
        o_ref[rows, :] = x_ref[rows, :] + (h * _rms_scale(h)) * g_ref[...]
    _for_row_chunks(o_ref.shape[0], body, OUT_ROWS)


def _outproj(x, ypl, ya, g, w_out, layer):
    half = POOL_WIDTH + LRU_WIDTH
    vmem = (4 * OUT_TM * D_MODEL * 4 + 4 * OUT_TM * half * 2 + 2 * D_MODEL * D_MODEL * 2
            + 4 * OUT_TM * D_MODEL * 4)
    return pl.pallas_call(
        _outproj_kernel,
        out_shape=jax.ShapeDtypeStruct((SEQ, D_MODEL), F32),
        grid=(SEQ // OUT_TM,),
        in_specs=[
            pl.BlockSpec((OUT_TM, D_MODEL), lambda i: (i, 0)),
            pl.BlockSpec((OUT_TM, half), lambda i: (i, 0)),
            pl.BlockSpec((OUT_TM, ATTN_WIDTH), lambda i: (i, 0)),
            pl.BlockSpec((1, D_MODEL), lambda i: (0, 0)),
            pl.BlockSpec((None, D_MODEL, D_MODEL), lambda i: (layer, 0, 0)),
        ],
        out_specs=pl.BlockSpec((OUT_TM, D_MODEL), lambda i: (i, 0)),
        compiler_params=pltpu.CompilerParams(
            dimension_semantics=("parallel",),
            vmem_limit_bytes=_vmem_limit(vmem)),
        name="mix_outproj",
    )(x, ypl, ya, g, w_out)


def _cast_kernel(w_ref, o_ref):
    o_ref[...] = w_ref[...].astype(BF16)


def _scaled_cast_kernel(w_ref, s_ref, o_ref):
    o_ref[...] = (w_ref[...] * s_ref[...]).astype(BF16)


def _to_bf16(w, col_scale=None):
    depth, r, c = w.shape
    rows = depth * r
    tr = rows
    while tr * c * 4 > CAST_BLOCK_BYTES:
        tr //= 2
    w2 = w.reshape(rows, c)
    in_specs = [pl.BlockSpec((tr, c), lambda i: (i, 0))]
    args = [w2]
    body = _cast_kernel
    if col_scale is not None:
        in_specs.append(pl.BlockSpec((1, c), lambda i: (0, 0)))
        args.append(col_scale.reshape(1, c))
        body = _scaled_cast_kernel
    out = pl.pallas_call(
        body,
        out_shape=jax.ShapeDtypeStruct((rows, c), BF16),
        grid=(rows // tr,),
        in_specs=in_specs,
        out_specs=pl.BlockSpec((tr, c), lambda i: (i, 0)),
        compiler_params=pltpu.CompilerParams(
            dimension_semantics=("parallel",),
            vmem_limit_bytes=_vmem_limit(2 * tr * c * (4 + 2) + 2 * tr * c * 4)),
        name="cast_bf16",
    )(*args)
    return out.reshape(depth, r, c)


def _block_diag(w):
    heads, c, _ = w.shape
    eye = jnp.eye(heads, dtype=w.dtype)
    return (eye[:, None, :, None] * w[:, :, None, :]).reshape(heads * c, heads * c)


def kernel(x, norm_g, ffn1_w_in, ffn1_w_out, mix_w_in, mix_w_out, pool_w, pool_scale,
           lru_conv_w, lru_conv_b, lru_gate_w, lru_gate_b, lru_lambda, ffn2_w_in, ffn2_w_out):
    xs = x.reshape(SEQ, D_MODEL)
    w1_in = _to_bf16(ffn1_w_in)
    w1_out = _to_bf16(ffn1_w_out)
    w2_in = _to_bf16(ffn2_w_in)
    w2_out = _to_bf16(ffn2_w_out)
    qscale = jnp.concatenate([jnp.ones((PL_WIDTH,), F32),
                              jnp.full((ATTN_WIDTH,), ATTN_HEAD_DIM ** -0.5, F32),
                              jnp.ones((2 * ATTN_WIDTH,), F32)])
    wm_in = _to_bf16(mix_w_in, qscale)
    wm_out = _to_bf16(mix_w_out)
    pw = pool_w.astype(BF16)
    for layer in range(DEPTH):
        g = norm_g[layer].reshape(6, 1, D_MODEL)
        xs = _ffn(xs, g[0], g[1], w1_in, w1_out, layer)
        zpl, qkv, qkv16 = _inproj(xs, g[2], wm_in, layer)
        ypl = _mix(zpl, pw[layer], pool_scale[layer].reshape(1, POOL_WIDTH),
                   lru_conv_w[layer], lru_conv_b[layer].reshape(1, LRU_WIDTH),
                   _block_diag(lru_gate_w[layer, 0]).astype(BF16),
                   _block_diag(lru_gate_w[layer, 1]).astype(BF16),
                   lru_gate_b[layer, 0].reshape(1, LRU_WIDTH),
                   lru_gate_b[layer, 1].reshape(1, LRU_WIDTH),
                   lru_lambda[layer].reshape(1, LRU_WIDTH))
        ya = _attention(qkv, qkv16)
        xs = _outproj(xs, ypl, ya, g[3], wm_out, layer)
        xs = _ffn(xs, g[4], g[5], w2_in, w2_out, layer)
    return xs.reshape(x.shape)
```

```python
import jax
import jax.numpy as jnp
from jax import lax
from jax.experimental import pallas as pl
from jax.experimental.pallas import tpu as pltpu

F32 = jnp.float32
BF16 = jnp.bfloat16

D_MODEL = 2048
SEQ = 16384
DEPTH = 2
D_FF = 5632
POOL_WIDTH = 512
POOL_WINDOWS = (2, 4, 8, 16)
POOL_GROUP = 128
LRU_WIDTH = 512
LRU_HEADS = 8
LRU_HEAD_DIM = 64
LRU_CONV = 4
LRU_C = 8.0
ATTN_WIDTH = 1024
ATTN_HEAD_DIM = 64
ATTN_HEADS = 16
ATTN_DILATIONS = (1, 4, 16)
ATTN_BLOCK = 128
PL_WIDTH = POOL_WIDTH + 2 * LRU_WIDTH
IN_WIDTH = PL_WIDTH + 3 * ATTN_WIDTH
NORM_EPS = 1e-6
NEG_INF = -1e30

V7X_LANES = 128
VMEM_LIMIT_CAP = 60000 * 1024

FFN_TM = 1024
FFN_TF = 512
PROJ_TM = 1024
PROJ_TN = 768
OUT_TM = 512
OUT_ROWS = 256
MIX_T = 512
MIX_HALO = 16
DIL_PLANES = 16
DIL_STEP = 4
ATTN_TILE = DIL_PLANES * ATTN_BLOCK
ATTN_W = 256
HEAD_PAIRS = ATTN_W // V7X_LANES
ATTN_GROUP = 4
QUARTER = ATTN_BLOCK // 4
COMBINE_ROWS = 256
NORM_ROWS = 128
APPLY_UNROLL = 2
SCALE_UNROLL = 4
CAST_BLOCK_BYTES = 8 * 1024 * 1024


def _vmem_limit(nbytes):
    return int(min(VMEM_LIMIT_CAP, nbytes))


def _rms_scale(x):
    return lax.rsqrt(jnp.mean(x * x, axis=-1, keepdims=True) + NORM_EPS)


def _for_row_chunks(n_rows, body, chunk=NORM_ROWS, unroll=1):
    def step(c, carry):
        body(pl.ds(pl.multiple_of(c * chunk, chunk), chunk))
        return carry
    lax.fori_loop(0, n_rows // chunk, step, 0, unroll=unroll)


def _row_scales(src_ref, rs_ref, zero_ref=None):
    def body(rows):
        rs_ref[rows, :] = _rms_scale(src_ref[rows, :])
        if zero_ref is not None:
            zero_ref[rows, :] = jnp.zeros((NORM_ROWS, zero_ref.shape[1]), zero_ref.dtype)
    _for_row_chunks(src_ref.shape[0], body, unroll=SCALE_UNROLL)


def _norm_to_bf16(x_ref, g_ref, xn_ref, rs_ref, zero_ref=None):
    _row_scales(x_ref, rs_ref, zero_ref)

    def body(rows):
        xn_ref[rows, :] = ((x_ref[rows, :] * rs_ref[rows, :]) * g_ref[...]).astype(BF16)
    _for_row_chunks(x_ref.shape[0], body, unroll=APPLY_UNROLL)


def _ffn_kernel(x_ref, gin_ref, gout_ref, wg_ref, wu_ref, wo_ref, o_ref, xn_ref, rs_ref):
    f = pl.program_id(1)

    @pl.when(f == 0)
    def _():
        _norm_to_bf16(x_ref, gin_ref, xn_ref, rs_ref, zero_ref=o_ref)

    xn = xn_ref[...]
    gate = jnp.dot(xn, wg_ref[...], preferred_element_type=F32)
    up = jnp.dot(xn, wu_ref[...], preferred_element_type=F32)
    h = ((gate * jax.nn.sigmoid(gate)) * up).astype(BF16)
    o_ref[...] += jnp.dot(h, wo_ref[...], preferred_element_type=F32)

    @pl.when(f == pl.num_programs(1) - 1)
    def _():
        _row_scales(o_ref, rs_ref)

        def body(rows):
            half_g = 0.5 * gout_ref[...]
            o_ref[rows, :] = x_ref[rows, :] + (o_ref[rows, :] * rs_ref[rows, :]) * half_g
        _for_row_chunks(o_ref.shape[0], body, unroll=APPLY_UNROLL)


def _ffn(x, g_in, g_out, w_in, w_out, layer):
    nf = D_FF // FFN_TF
    vmem = (4 * FFN_TM * D_MODEL * 4
            + FFN_TM * D_MODEL * 2
            + 2 * 3 * D_MODEL * FFN_TF * 2
            + 8 * FFN_TM * FFN_TF * 4)
    return pl.pallas_call(
        _ffn_kernel,
        out_shape=jax.ShapeDtypeStruct((SEQ, D_MODEL), F32),
        grid=(SEQ // FFN_TM, nf),
        in_specs=[
            pl.BlockSpec((FFN_TM, D_MODEL), lambda i, f: (i, 0)),
            pl.BlockSpec((1, D_MODEL), lambda i, f: (0, 0)),
            pl.BlockSpec((1, D_MODEL), lambda i, f: (0, 0)),
            pl.BlockSpec((None, D_MODEL, FFN_TF), lambda i, f: (layer, 0, f)),
            pl.BlockSpec((None, D_MODEL, FFN_TF), lambda i, f: (layer, 0, f + nf)),
            pl.BlockSpec((None, FFN_TF, D_MODEL), lambda i, f: (layer, f, 0)),
        ],
        out_specs=pl.BlockSpec((FFN_TM, D_MODEL), lambda i, f: (i, 0)),
        scratch_shapes=[pltpu.VMEM((FFN_TM, D_MODEL), BF16), pltpu.VMEM((FFN_TM, 1), F32)],
        compiler_params=pltpu.CompilerParams(
            dimension_semantics=("parallel", "arbitrary"),
            vmem_limit_bytes=_vmem_limit(vmem)),
        name="ffn",
    )(x, g_in, g_out, w_in, w_in, w_out)


PL_TILES = PL_WIDTH // PROJ_TN


def _inproj_kernel(x_ref, g_ref, w_ref, zpl_ref, qkv_ref, qkv16_ref, xn_ref, rs_ref, zs_ref, zt_ref):
    n = pl.program_id(1)

    @pl.when(n == 0)
    def _():
        _norm_to_bf16(x_ref, g_ref, xn_ref, rs_ref)

    @pl.when(n < PL_TILES)
    def _():
        zpl_ref[...] = jnp.dot(xn_ref[...], w_ref[...], preferred_element_type=F32)

    @pl.when(n >= PL_TILES)
    def _():
        z = jnp.dot(xn_ref[...], w_ref[...], preferred_element_type=F32)
        qkv_ref[...] = z.astype(BF16)
        for c in range(PROJ_TN // V7X_LANES):
            lanes = slice(c * V7X_LANES, (c + 1) * V7X_LANES)
            zs_ref[c] = z[:, lanes]
            for r1 in range(DIL_STEP):
                zt_ref[c, r1] = zs_ref[c, pl.ds(r1, PROJ_TM // DIL_STEP, stride=DIL_STEP), :]
            for r1 in range(DIL_STEP):
                for r2 in range(DIL_STEP):
                    plane = zt_ref[c, r1, pl.ds(r2, PROJ_TM // DIL_PLANES, stride=DIL_STEP), :]
                    qkv16_ref[r1 + DIL_STEP * r2, :, lanes] = plane.astype(BF16)


def _inproj(x, g, w_in, layer):
    rows_per_plane = PROJ_TM // DIL_PLANES
    vmem = (2 * PROJ_TM * D_MODEL * 4 + PROJ_TM * D_MODEL * 2 + 2 * D_MODEL * PROJ_TN * 2
            + 2 * PROJ_TM * PROJ_TN * (4 + 2 + 2) + 4 * PROJ_TM * PROJ_TN * 4)
    qkv_col = lambda n: jnp.maximum(n - PL_TILES, 0)
    return pl.pallas_call(
        _inproj_kernel,
        out_shape=(jax.ShapeDtypeStruct((SEQ, PL_WIDTH), F32),
                   jax.ShapeDtypeStruct((SEQ, 3 * ATTN_WIDTH), BF16),
                   jax.ShapeDtypeStruct((DIL_PLANES, SEQ // DIL_PLANES, 3 * ATTN_WIDTH), BF16)),
        grid=(SEQ // PROJ_TM, IN_WIDTH // PROJ_TN),
        in_specs=[
            pl.BlockSpec((PROJ_TM, D_MODEL), lambda i, n: (i, 0)),
            pl.BlockSpec((1, D_MODEL), lambda i, n: (0, 0)),
            pl.BlockSpec((None, D_MODEL, PROJ_TN), lambda i, n: (layer, 0, n)),
        ],
        out_specs=(
            pl.BlockSpec((PROJ_TM, PROJ_TN), lambda i, n: (i, jnp.minimum(n, PL_TILES - 1))),
            pl.BlockSpec((PROJ_TM, PROJ_TN), lambda i, n: (i, qkv_col(n))),
            pl.BlockSpec((DIL_PLANES, rows_per_plane, PROJ_TN), lambda i, n: (0, i, qkv_col(n))),
        ),
        scratch_shapes=[pltpu.VMEM((PROJ_TM, D_MODEL), BF16),
                        pltpu.VMEM((PROJ_TM, 1), F32),
                        pltpu.VMEM((PROJ_TN // V7X_LANES, PROJ_TM, V7X_LANES), F32),
                        pltpu.VMEM((PROJ_TN // V7X_LANES, DIL_STEP, PROJ_TM // DIL_STEP, V7X_LANES), F32)],
        compiler_params=pltpu.CompilerParams(
            dimension_semantics=("parallel", "arbitrary"),
            vmem_limit_bytes=_vmem_limit(vmem)),
        name="mix_inproj",
    )(x, g, w_in)


def _shift_rows(v, k, fill):
    t = v.shape[0]
    if k % 8 == 0:
        pad = jnp.full((k, v.shape[1]), fill, v.dtype)
        return jnp.concatenate([pad, v[: t - k]], axis=0)
    rolled = pltpu.roll(v, k, 0)
    row = lax.broadcasted_iota(jnp.int32, v.shape, 0)
    return jnp.where(row >= k, rolled, fill)


def _mix_kernel(z_ref, pw_ref, ps_ref, cw_ref, cb_ref, wr_ref, wi_ref, br_ref, bi_ref,
                lam_ref, y_ref, ext_ref, h_ref):
    i = pl.program_id(0)
    T = MIX_T
    H = MIX_HALO

    @pl.when(i == 0)
    def _():
        ext_ref[0:H, :] = jnp.zeros((H, POOL_WIDTH + LRU_WIDTH), F32)
        h_ref[...] = jnp.zeros_like(h_ref)

    ext_ref[H:H + T, :] = z_ref[:, 0:POOL_WIDTH + LRU_WIDTH]

    pos = (i * T + 1 + lax.broadcasted_iota(jnp.int32, (T, POOL_GROUP), 0)).astype(F32)
    for g, win in enumerate(POOL_WINDOWS):
        lanes = slice(g * POOL_GROUP, (g + 1) * POOL_GROUP)
        e = ext_ref[:, lanes]
        s = e
        k = 1
        while k < win:
            s = s + pltpu.roll(s, k, 0)
            k *= 2
        u = e[H:H + T]
        pooled = s[H:H + T] / jnp.minimum(pos, float(win)) - u
        y = jnp.dot(pooled.astype(BF16), pw_ref[g], preferred_element_type=F32)
        y_ref[:, lanes] = (y * ps_ref[:, lanes]).astype(BF16)

    xl = slice(POOL_WIDTH, POOL_WIDTH + LRU_WIDTH)
    xc = cb_ref[...] + cw_ref[LRU_CONV - 1:LRU_CONV, :] * ext_ref[H:H + T, xl]
    for k in range(1, LRU_CONV):
        xc = xc + cw_ref[LRU_CONV - 1 - k:LRU_CONV - k, :] * ext_ref[H - k:H - k + T, xl]
    xcb = xc.astype(BF16)
    r = jax.nn.sigmoid(jnp.dot(xcb, wr_ref[...], preferred_element_type=F32) + br_ref[...])
    ig = jax.nn.sigmoid(jnp.dot(xcb, wi_ref[...], preferred_element_type=F32) + bi_ref[...])
    nlam = -lam_ref[...]
    softplus = jnp.maximum(nlam, 0.0) + jnp.log1p(jnp.exp(-jnp.abs(nlam)))
    log_a = (-LRU_C * r) * softplus
    a = jnp.exp(log_a)
    b = jnp.sqrt(-jnp.tanh(log_a) * (a * a + 1.0)) * (ig * xc)
    k = 1
    while k < T:
        b = a * _shift_rows(b, k, 0.0) + b
        a = a * _shift_rows(a, k, 1.0)
        k *= 2
    h = a * h_ref[0:1, :] + b
    h_ref[0:1, :] = h[T - 1:T, :]
    gate_in = z_ref[:, POOL_WIDTH + LRU_WIDTH:PL_WIDTH]
    y_ref[:, POOL_WIDTH:POOL_WIDTH + LRU_WIDTH] = (h * jax.nn.gelu(gate_in)).astype(BF16)

    ext_ref[0:H, :] = ext_ref[T:T + H, :]


def _mix(zpl, pool_w, pool_scale, conv_w, conv_b, w_r, w_i, b_r, b_i, lam):
    full = lambda shape: pl.BlockSpec(shape, lambda i: (0,) * len(shape))
    vmem = (2 * MIX_T * PL_WIDTH * 4 + 2 * MIX_T * 1024 * 2 + (MIX_T + MIX_HALO) * 1024 * 4
            + 4 * 512 * 512 * 2 + 24 * MIX_T * LRU_WIDTH * 4)
    return pl.pallas_call(
        _mix_kernel,
        out_shape=jax.ShapeDtypeStruct((SEQ, POOL_WIDTH + LRU_WIDTH), BF16),
        grid=(SEQ // MIX_T,),
        in_specs=[
            pl.BlockSpec((MIX_T, PL_WIDTH), lambda i: (i, 0)),
            full((len(POOL_WINDOWS), POOL_GROUP, POOL_GROUP)),
            full((1, POOL_WIDTH)),
            full((LRU_CONV, LRU_WIDTH)),
            full((1, LRU_WIDTH)),
            full((LRU_WIDTH, LRU_WIDTH)),
            full((LRU_WIDTH, LRU_WIDTH)),
            full((1, LRU_WIDTH)),
            full((1, LRU_WIDTH)),
            full((1, LRU_WIDTH)),
        ],
        out_specs=pl.BlockSpec((MIX_T, POOL_WIDTH + LRU_WIDTH), lambda i: (i, 0)),
        scratch_shapes=[pltpu.VMEM((MIX_T + MIX_HALO, POOL_WIDTH + LRU_WIDTH), F32),
                        pltpu.VMEM((8, LRU_WIDTH), F32)],
        compiler_params=pltpu.CompilerParams(
            dimension_semantics=("arbitrary",),
            vmem_limit_bytes=_vmem_limit(vmem)),
        name="pool_lru",
    )(zpl, pool_w, pool_scale, conv_w, conv_b, w_r, w_i, b_r, b_i, lam)


_CONTRACT_LANES = (((1,), (1,)), ((), ()))
_CONTRACT_ROWS = (((0,), (0,)), ((), ()))


def _mod_pow2(x, n):
    assert n & (n - 1) == 0
    return jnp.bitwise_and(x, n - 1)


def _div_pow2(x, n):
    assert n & (n - 1) == 0
    return jnp.right_shift(x, n.bit_length() - 1)


def _build_bias(hg, bias_s):
    B = ATTN_BLOCK
    u = lax.broadcasted_iota(jnp.int32, (2 * B, 2 * B), 0)
    col = lax.broadcasted_iota(jnp.int32, (2 * B, 2 * B), 1)
    qi = _mod_pow2(col, B)
    heads_per_step = ATTN_W // ATTN_HEAD_DIM
    step_ratio = 2.0 ** (-8.0 * heads_per_step / ATTN_HEADS)
    group_scale = jnp.float32(1.0)
    for grp in range(1, ATTN_WIDTH // ATTN_W):
        group_scale = jnp.where(hg == grp, step_ratio ** grp, group_scale)
    for bi, dil in enumerate(ATTN_DILATIONS):
        if dil == 4:
            delta = (4 * (_mod_pow2(qi, QUARTER) - _mod_pow2(u, 2 * QUARTER) + QUARTER)
                     + _div_pow2(qi, QUARTER) - _div_pow2(u, 2 * QUARTER))
            prev = _mod_pow2(u, 2 * QUARTER) < QUARTER
        else:
            delta = qi - u + B
            prev = u < B
        valid = jnp.logical_and(delta >= 0, delta <= B)
        dist = (delta * dil).astype(F32)
        for hp in range(HEAD_PAIRS):
            slope_a = 2.0 ** (-8.0 * (2 * hp + 1) / ATTN_HEADS)
            slope_b = 2.0 ** (-8.0 * (2 * hp + 2) / ATTN_HEADS)
            slope = group_scale * jnp.where(col < B, slope_a, slope_b)
            bias = (-slope) * dist
            bias_s[bi, 0, hp] = jnp.where(valid, bias, NEG_INF)
            bias_s[bi, 1, hp] = jnp.where(jnp.logical_and(valid, jnp.logical_not(prev)), bias, NEG_INF)


def _scores_stage(q_pair, kcat, bias, s_ref, m_ref, slot):
    low = lax.broadcasted_iota(jnp.int32, q_pair.shape, 1) < ATTN_HEAD_DIM
    zero = jnp.zeros_like(q_pair)
    qcat = jnp.concatenate([jnp.where(low, q_pair, zero), jnp.where(low, zero, q_pair)], axis=0)
    s = lax.dot_general(kcat, qcat, _CONTRACT_LANES, preferred_element_type=F32) + bias
    s_ref[slot] = s
    m_ref[slot, 0:1, :] = jnp.max(s, axis=0, keepdims=True)


def _probs_stage(vcat, s_ref, m_ref, ot_ref, ml_ref, slot):
    m = m_ref[slot, 0:1, :]
    p = jnp.exp(s_ref[slot] - m)
    ot_ref[slot] = lax.dot_general(vcat, p.astype(BF16), _CONTRACT_ROWS, preferred_element_type=F32)
    ml_ref[slot, 0:1, :] = m
    ml_ref[slot, 1:2, :] = jnp.sum(p, axis=0, keepdims=True)


def _finish_stage(ot_ref, ml_ref, slot):
    B = ATTN_BLOCK
    H = ATTN_HEAD_DIM
    ot = ot_ref[slot]
    m = ml_ref[slot, 0:1, :]
    l = ml_ref[slot, 1:2, :]
    inv = 1.0 / l
    lse = m + jnp.log(l)
    o_t = jnp.concatenate([ot[0:H, 0:B] * inv[:, 0:B], ot[H:2 * H, B:2 * B] * inv[:, B:2 * B]], axis=0)
    l_t = jnp.concatenate([jnp.broadcast_to(lse[:, 0:B], (H, B)),
                           jnp.broadcast_to(lse[:, B:2 * B], (H, B))], axis=0)
    return o_t.T, l_t.T


def _software_pipeline(n, scores, probs, finish):
    scores(0)
    probs(0)
    scores(1)

    def body(t, carry):
        finish(t - 2)
        probs(t - 1)
        scores(t)
        return carry
    lax.fori_loop(2, n, body, 0)
    finish(n - 2)
    probs(n - 1)
    finish(n - 1)


def _attn_kernel(qn, kn, vn, q16, k16, v16, y_ref, kxn, vxn, kx16, vx16, o_s, l_s, bias_s, *stage):
    hg = pl.program_id(0)
    j = pl.program_id(1)
    B = ATTN_BLOCK
    T = ATTN_TILE
    first = jnp.where(j == 0, 1, 0)

    @pl.when(j == 0)
    def _():
        kxn[0:B, :] = jnp.zeros((B, ATTN_W), BF16)
        vxn[0:B, :] = jnp.zeros((B, ATTN_W), BF16)
        kx16[:, 0:B, :] = jnp.zeros((DIL_PLANES, B, ATTN_W), BF16)
        vx16[:, 0:B, :] = jnp.zeros((DIL_PLANES, B, ATTN_W), BF16)
        _build_bias(hg, bias_s)

    kxn[B:B + T, :] = kn[...]
    vxn[B:B + T, :] = vn[...]
    kx16[:, B:2 * B, :] = k16[...]
    vx16[:, B:2 * B, :] = v16[...]

    def lanes_of(hp):
        return slice(hp * V7X_LANES, (hp + 1) * V7X_LANES)

    def aligned(x, m):
        return x if isinstance(x, int) else pl.multiple_of(x, m)

    def run_branch(bi, fetch_q, fetch_k, fetch_v, variant_of, scatter):
        def units(t):
            for i in range(ATTN_GROUP):
                for hp in range(HEAD_PAIRS):
                    yield t * ATTN_GROUP + i, hp, stage[4 * (i * HEAD_PAIRS + hp):4 * (i * HEAD_PAIRS + hp) + 4]

        def scores(t):
            for blk, hp, (s_ref, m_ref, _, _) in units(t):
                _scores_stage(fetch_q(blk, hp), fetch_k(blk, hp), bias_s[bi, variant_of(blk), hp],
                              s_ref, m_ref, t % 2)

        def probs(t):
            for blk, hp, (s_ref, m_ref, ot_ref, ml_ref) in units(t):
                _probs_stage(fetch_v(blk, hp), s_ref, m_ref, ot_ref, ml_ref, t % 2)

        def finish(t):
            for blk, hp, (_, _, ot_ref, ml_ref) in units(t):
                o, l = _finish_stage(ot_ref, ml_ref, t % 2)
                scatter(blk, hp, o, l)

        _software_pipeline(DIL_PLANES // ATTN_GROUP, scores, probs, finish)

    def d1_rows(blk, n):
        return pl.ds(aligned(blk * B, B), n)

    def d1_scatter(blk, hp, o, l):
        o_s[0, hp, d1_rows(blk, B), :] = o
        l_s[0, hp, d1_rows(blk, B), :] = l

    run_branch(0,
               lambda blk, hp: qn[d1_rows(blk, B), lanes_of(hp)],
               lambda blk, hp: kxn[d1_rows(blk, 2 * B), lanes_of(hp)],
               lambda blk, hp: vxn[d1_rows(blk, 2 * B), lanes_of(hp)],
               lambda blk: jnp.where(blk == 0, first, 0),
               d1_scatter)

    per_plane = ATTN_BLOCK // QUARTER

    def d4_gather(ref, blk, hp, row0, n):
        c = blk // per_plane
        rows = pl.ds(aligned(row0 + (blk % per_plane) * QUARTER, QUARTER), n)
        return jnp.concatenate([ref[per_plane * a + c, rows, lanes_of(hp)] for a in range(per_plane)], axis=0)

    def d4_scatter(blk, hp, o, l):
        c = blk // per_plane
        b = blk % per_plane
        for a in range(per_plane):
            rows = pl.ds(b * QUARTER * DIL_PLANES + per_plane * a + c, QUARTER, stride=DIL_PLANES)
            o_s[1, hp, rows, :] = o[a * QUARTER:(a + 1) * QUARTER]
            l_s[1, hp, rows, :] = l[a * QUARTER:(a + 1) * QUARTER]

    run_branch(1,
               lambda blk, hp: d4_gather(q16, blk, hp, 0, QUARTER),
               lambda blk, hp: d4_gather(kx16, blk, hp, B - QUARTER, 2 * QUARTER),
               lambda blk, hp: d4_gather(vx16, blk, hp, B - QUARTER, 2 * QUARTER),
               lambda blk: jnp.where(blk % per_plane == 0, first, 0),
               d4_scatter)

    def d16_scatter(blk, hp, o, l):
        rows = pl.ds(blk, B, stride=DIL_PLANES)
        o_s[2, hp, rows, :] = o
        l_s[2, hp, rows, :] = l

    run_branch(2,
               lambda blk, hp: q16[blk, :, lanes_of(hp)],
               lambda blk, hp: kx16[blk, :, lanes_of(hp)],
               lambda blk, hp: vx16[blk, :, lanes_of(hp)],
               lambda blk: first,
               d16_scatter)

    def merge(c, carry):
        rows = pl.ds(pl.multiple_of(c * COMBINE_ROWS, COMBINE_ROWS), COMBINE_ROWS)
        for hp in range(HEAD_PAIRS):
            l1 = l_s[0, hp, rows, :]
            l4 = l_s[1, hp, rows, :]
            l16 = l_s[2, hp, rows, :]
            top = jnp.maximum(jnp.maximum(l1, l4), l16)
            w1 = jnp.exp(l1 - top)
            w4 = jnp.exp(l4 - top)
            w16 = jnp.exp(l16 - top)
            num = w1 * o_s[0, hp, rows, :] + w4 * o_s[1, hp, rows, :] + w16 * o_s[2, hp, rows, :]
            y_ref[rows, lanes_of(hp)] = (num / (w1 + w4 + w16)).astype(BF16)
        return carry
    lax.fori_loop(0, T // COMBINE_ROWS, merge, 0)

    kxn[0:B, :] = kxn[T:T + B, :]
    vxn[0:B, :] = vxn[T:T + B, :]
    kx16[:, 0:B, :] = kx16[:, B:2 * B, :]
    vx16[:, 0:B, :] = vx16[:, B:2 * B, :]


def _attention(qkv, qkv16):
    B = ATTN_BLOCK
    ngrp = ATTN_WIDTH // ATTN_W
    nat = lambda which: pl.BlockSpec((ATTN_TILE, ATTN_W), lambda hg, j: (j, which * ngrp + hg))
    planes = lambda which: pl.BlockSpec((DIL_PLANES, B, ATTN_W), lambda hg, j: (0, j, which * ngrp + hg))
    vmem = (2 * 6 * ATTN_TILE * ATTN_W * 2 + 2 * ATTN_TILE * ATTN_W * 2
            + 2 * (ATTN_TILE + B) * ATTN_W * 2 + 2 * DIL_PLANES * 2 * B * ATTN_W * 2
            + 2 * 3 * HEAD_PAIRS * ATTN_TILE * V7X_LANES * 4
            + 3 * 2 * HEAD_PAIRS * 4 * B * B * 4
            + 16 * 1024 * 1024)
    return pl.pallas_call(
        _attn_kernel,
        out_shape=jax.ShapeDtypeStruct((SEQ, ATTN_WIDTH), BF16),
        grid=(ngrp, SEQ // ATTN_TILE),
        in_specs=[nat(0), nat(1), nat(2), planes(0), planes(1), planes(2)],
        out_specs=pl.BlockSpec((ATTN_TILE, ATTN_W), lambda hg, j: (j, hg)),
        scratch_shapes=[
            pltpu.VMEM((ATTN_TILE + B, ATTN_W), BF16),
            pltpu.VMEM((ATTN_TILE + B, ATTN_W), BF16),
            pltpu.VMEM((DIL_PLANES, 2 * B, ATTN_W), BF16),
            pltpu.VMEM((DIL_PLANES, 2 * B, ATTN_W), BF16),
            pltpu.VMEM((3, HEAD_PAIRS, ATTN_TILE, V7X_LANES), F32),
            pltpu.VMEM((3, HEAD_PAIRS, ATTN_TILE, V7X_LANES), F32),
            pltpu.VMEM((3, 2, HEAD_PAIRS, 2 * B, 2 * B), F32),
        ] + [
            pltpu.VMEM(shape, F32)
            for _ in range(ATTN_GROUP * HEAD_PAIRS)
            for shape in ((2, 2 * B, 2 * B), (2, 8, 2 * B), (2, B, 2 * B), (2, 8, 2 * B))
        ],
        compiler_params=pltpu.CompilerParams(
            dimension_semantics=("arbitrary", "arbitrary"),
            vmem_limit_bytes=_vmem_limit(vmem)),
        name="dilated_attn",
    )(qkv, qkv, qkv, qkv16, qkv16, qkv16)


def _outproj_kernel(x_ref, ypl_ref, ya_ref, g_ref, w_ref, o_ref):
    half = POOL_WIDTH + LRU_WIDTH

    def body(rows):
        h = (jnp.dot(ypl_ref[rows, :], w_ref[0:half, :], preferred_element_type=F32)
             + jnp.dot(ya_ref[rows, :], w_ref[half:2 * half, :], preferred_element_type=F32))
        o_ref[rows, :] = x_ref[rows, :] + (h * _rms_scale(h)) * g_ref[...]
    _for_row_chunks(o_ref.shape[0], body, OUT_ROWS)


def _outproj(x, ypl, ya, g, w_out, layer):
    half = POOL_WIDTH + LRU_WIDTH
    vmem = (4 * OUT_TM * D_MODEL * 4 + 4 * OUT_TM * half * 2 + 2 * D_MODEL * D_MODEL * 2
            + 4 * OUT_TM * D_MODEL * 4)
    return pl.pallas_call(
        _outproj_kernel,
        out_shape=jax.ShapeDtypeStruct((SEQ, D_MODEL), F32),
        grid=(SEQ // OUT_TM,),
        in_specs=[
            pl.BlockSpec((OUT_TM, D_MODEL), lambda i: (i, 0)),
            pl.BlockSpec((OUT_TM, half), lambda i: (i, 0)),
            pl.BlockSpec((OUT_TM, ATTN_WIDTH), lambda i: (i, 0)),
            pl.BlockSpec((1, D_MODEL), lambda i: (0, 0)),
            pl.BlockSpec((None, D_MODEL, D_MODEL), lambda i: (layer, 0, 0)),
        ],
        out_specs=pl.BlockSpec((OUT_TM, D_MODEL), lambda i: (i, 0)),
        compiler_params=pltpu.CompilerParams(
            dimension_semantics=("parallel",),
            vmem_limit_bytes=_vmem_limit(vmem)),
        name="mix_outproj",
    )(x, ypl, ya, g, w_out)


def _cast_kernel(w_ref, o_ref):
    o_ref[...] = w_ref[...].astype(BF16)


def _scaled_cast_kernel(w_ref, s_ref, o_ref):
    o_ref[...] = (w_ref[...] * s_ref[...]).astype(BF16)


def _to_bf16(w, col_scale=None):
    depth, r, c = w.shape
    rows = depth * r
    tr = rows
    while tr * c * 4 > CAST_BLOCK_BYTES:
        tr //= 2
    w2 = w.reshape(rows, c)
    in_specs = [pl.BlockSpec((tr, c), lambda i: (i, 0))]
    args = [w2]
    body = _cast_kernel
    if col_scale is not None:
        in_specs.append(pl.BlockSpec((1, c), lambda i: (0, 0)))
        args.append(col_scale.reshape(1, c))
        body = _scaled_cast_kernel
    out = pl.pallas_call(
        body,
        out_shape=jax.ShapeDtypeStruct((rows, c), BF16),
        grid=(rows // tr,),
        in_specs=in_specs,
        out_specs=pl.BlockSpec((tr, c), lambda i: (i, 0)),
        compiler_params=pltpu.CompilerParams(
            dimension_semantics=("parallel",),
            vmem_limit_bytes=_vmem_limit(2 * tr * c * (4 + 2) + 2 * tr * c * 4)),
        name="cast_bf16",
    )(*args)
    return out.reshape(depth, r, c)


def _block_diag(w):
    heads, c, _ = w.shape
    eye = jnp.eye(heads, dtype=w.dtype)
    return (eye[:, None, :, None] * w[:, :, None, :]).reshape(heads * c, heads * c)


def kernel(x, norm_g, ffn1_w_in, ffn1_w_out, mix_w_in, mix_w_out, pool_w, pool_scale,
           lru_conv_w, lru_conv_b, lru_gate_w, lru_gate_b, lru_lambda, ffn2_w_in, ffn2_w_out):
    xs = x.reshape(SEQ, D_MODEL)
    w1_in = _to_bf16(ffn1_w_in)
    w1_out = _to_bf16(ffn1_w_out)
    w2_in = _to_bf16(ffn2_w_in)
    w2_out = _to_bf16(ffn2_w_out)
    qscale = jnp.concatenate([jnp.ones((PL_WIDTH,), F32),
                              jnp.full((ATTN_WIDTH,), ATTN_HEAD_DIM ** -0.5, F32),
                              jnp.ones((2 * ATTN_WIDTH,), F32)])
    wm_in = _to_bf16(mix_w_in, qscale)
    wm_out = _to_bf16(mix_w_out)
    pw = pool_w.astype(BF16)
    for layer in range(DEPTH):
        g = norm_g[layer].reshape(6, 1, D_MODEL)
        xs = _ffn(xs, g[0], g[1], w1_in, w1_out, layer)
        zpl, qkv, qkv16 = _inproj(xs, g[2], wm_in, layer)
        ypl = _mix(zpl, pw[layer], pool_scale[layer].reshape(1, POOL_WIDTH),
                   lru_conv_w[layer], lru_conv_b[layer].reshape(1, LRU_WIDTH),
                   _block_diag(lru_gate_w[layer, 0]).astype(BF16),
                   _block_diag(lru_gate_w[layer, 1]).astype(BF16),
                   lru_gate_b[layer, 0].reshape(1, LRU_WIDTH),
                   lru_gate_b[layer, 1].reshape(1, LRU_WIDTH),
                   lru_lambda[layer].reshape(1, LRU_WIDTH))
        ya = _attention(qkv, qkv16)
        xs = _outproj(xs, ypl, ya, g[3], wm_out, layer)
        xs = _ffn(xs, g[4], g[5], w2_in, w2_out, layer)
    return xs.reshape(x.shape)
```

```python
import jax
import jax.numpy as jnp
from jax import lax
from jax.experimental import pallas as pl
from jax.experimental.pallas import tpu as pltpu

F32 = jnp.float32
BF16 = jnp.bfloat16

D_MODEL = 2048
SEQ = 16384
DEPTH = 2
D_FF = 5632
POOL_WIDTH = 512
POOL_WINDOWS = (2, 4, 8, 16)
POOL_GROUP = 128
LRU_WIDTH = 512
LRU_HEADS = 8
LRU_HEAD_DIM = 64
LRU_CONV = 4
LRU_C = 8.0
ATTN_WIDTH = 1024
ATTN_HEAD_DIM = 64
ATTN_HEADS = 16
ATTN_DILATIONS = (1, 4, 16)
ATTN_BLOCK = 128
PL_WIDTH = POOL_WIDTH + 2 * LRU_WIDTH
IN_WIDTH = PL_WIDTH + 3 * ATTN_WIDTH
NORM_EPS = 1e-6
NEG_INF = -1e30

V7X_LANES = 128
VMEM_LIMIT_CAP = 60000 * 1024

FFN_TM = 1024
FFN_TF = 512
PROJ_TM = 1024
PROJ_TN = 768
OUT_TM = 512
OUT_ROWS = 256
MIX_T = 512
MIX_HALO = 16
DIL_PLANES = 16
DIL_STEP = 4
ATTN_TILE = DIL_PLANES * ATTN_BLOCK
ATTN_W = 256
HEAD_PAIRS = ATTN_W // V7X_LANES
ATTN_GROUP = 4
QUARTER = ATTN_BLOCK // 4
COMBINE_ROWS = 256
NORM_ROWS = 128
APPLY_UNROLL = 2
SCALE_UNROLL = 4
CAST_BLOCK_BYTES = 8 * 1024 * 1024


def _vmem_limit(nbytes):
    return int(min(VMEM_LIMIT_CAP, nbytes))


def _rms_scale(x):
    return lax.rsqrt(jnp.mean(x * x, axis=-1, keepdims=True) + NORM_EPS)


def _for_row_chunks(n_rows, body, chunk=NORM_ROWS, unroll=1):
    def step(c, carry):
        body(pl.ds(pl.multiple_of(c * chunk, chunk), chunk))
        return carry
    lax.fori_loop(0, n_rows // chunk, step, 0, unroll=unroll)


def _row_scales(src_ref, rs_ref, zero_ref=None):
    def body(rows):
        rs_ref[rows, :] = _rms_scale(src_ref[rows, :])
        if zero_ref is not None:
            zero_ref[rows, :] = jnp.zeros((NORM_ROWS, zero_ref.shape[1]), zero_ref.dtype)
    _for_row_chunks(src_ref.shape[0], body, unroll=SCALE_UNROLL)


def _norm_to_bf16(x_ref, g_ref, xn_ref, rs_ref, zero_ref=None):
    _row_scales(x_ref, rs_ref, zero_ref)

    def body(rows):
        xn_ref[rows, :] = ((x_ref[rows, :] * rs_ref[rows, :]) * g_ref[...]).astype(BF16)
    _for_row_chunks(x_ref.shape[0], body, unroll=APPLY_UNROLL)


def _ffn_kernel(x_ref, gin_ref, gout_ref, wg_ref, wu_ref, wo_ref, o_ref, xn_ref, rs_ref):
    f = pl.program_id(1)

    @pl.when(f == 0)
    def _():
        _norm_to_bf16(x_ref, gin_ref, xn_ref, rs_ref, zero_ref=o_ref)

    xn = xn_ref[...]
    gate = jnp.dot(xn, wg_ref[...], preferred_element_type=F32)
    up = jnp.dot(xn, wu_ref[...], preferred_element_type=F32)
    h = ((gate * jax.nn.sigmoid(gate)) * up).astype(BF16)
    o_ref[...] += jnp.dot(h, wo_ref[...], preferred_element_type=F32)

    @pl.when(f == pl.num_programs(1) - 1)
    def _():
        _row_scales(o_ref, rs_ref)

        def body(rows):
            half_g = 0.5 * gout_ref[...]
            o_ref[rows, :] = x_ref[rows, :] + (o_ref[rows, :] * rs_ref[rows, :]) * half_g
        _for_row_chunks(o_ref.shape[0], body, unroll=APPLY_UNROLL)


def _ffn(x, g_in, g_out, w_in, w_out, layer):
    nf = D_FF // FFN_TF
    vmem = (4 * FFN_TM * D_MODEL * 4
            + FFN_TM * D_MODEL * 2
            + 2 * 3 * D_MODEL * FFN_TF * 2
            + 8 * FFN_TM * FFN_TF * 4)
    return pl.pallas_call(
        _ffn_kernel,
        out_shape=jax.ShapeDtypeStruct((SEQ, D_MODEL), F32),
        grid=(SEQ // FFN_TM, nf),
        in_specs=[
            pl.BlockSpec((FFN_TM, D_MODEL), lambda i, f: (i, 0)),
            pl.BlockSpec((1, D_MODEL), lambda i, f: (0, 0)),
            pl.BlockSpec((1, D_MODEL), lambda i, f: (0, 0)),
            pl.BlockSpec((None, D_MODEL, FFN_TF), lambda i, f: (f, layer, 0)),
            pl.BlockSpec((None, D_MODEL, FFN_TF), lambda i, f: (f + nf, layer, 0)),
            pl.BlockSpec((None, FFN_TF, D_MODEL), lambda i, f: (0, layer * nf + f, 0)),
        ],
        out_specs=pl.BlockSpec((FFN_TM, D_MODEL), lambda i, f: (i, 0)),
        scratch_shapes=[pltpu.VMEM((FFN_TM, D_MODEL), BF16), pltpu.VMEM((FFN_TM, 1), F32)],
        compiler_params=pltpu.CompilerParams(
            dimension_semantics=("parallel", "arbitrary"),
            vmem_limit_bytes=_vmem_limit(vmem)),
        name="ffn",
    )(x, g_in, g_out, w_in, w_in, w_out)


PL_TILES = PL_WIDTH // PROJ_TN


def _inproj_kernel(x_ref, g_ref, w_ref, zpl_ref, qkv_ref, qkv16_ref, xn_ref, rs_ref, zs_ref, zt_ref):
    n = pl.program_id(1)

    @pl.when(n == 0)
    def _():
        _norm_to_bf16(x_ref, g_ref, xn_ref, rs_ref)

    @pl.when(n < PL_TILES)
    def _():
        zpl_ref[...] = jnp.dot(xn_ref[...], w_ref[...], preferred_element_type=F32)

    @pl.when(n >= PL_TILES)
    def _():
        z = jnp.dot(xn_ref[...], w_ref[...], preferred_element_type=F32)
        _store_col_blocks(z.astype(BF16), qkv_ref)
        per_slab = ATTN_W // V7X_LANES
        for c in range(PROJ_TN // V7X_LANES):
            zs_ref[c] = z[:, c * V7X_LANES:(c + 1) * V7X_LANES]
            for r1 in range(DIL_STEP):
                zt_ref[c, r1] = zs_ref[c, pl.ds(r1, PROJ_TM // DIL_STEP, stride=DIL_STEP), :]
            lanes = slice((c % per_slab) * V7X_LANES, (c % per_slab + 1) * V7X_LANES)
            for r1 in range(DIL_STEP):
                for r2 in range(DIL_STEP):
                    plane = zt_ref[c, r1, pl.ds(r2, PROJ_TM // DIL_PLANES, stride=DIL_STEP), :]
                    qkv16_ref[c // per_slab, r1 + DIL_STEP * r2, :, lanes] = plane.astype(BF16)


def _inproj(x, g, w_in, layer):
    rows_per_plane = PROJ_TM // DIL_PLANES
    vmem = (2 * PROJ_TM * D_MODEL * 4 + PROJ_TM * D_MODEL * 2 + 2 * D_MODEL * PROJ_TN * 2
            + 2 * PROJ_TM * PROJ_TN * (4 + 2 + 2) + 4 * PROJ_TM * PROJ_TN * 4)
    qkv_col = lambda n: jnp.maximum(n - PL_TILES, 0)
    slabs = 3 * ATTN_WIDTH // ATTN_W
    slabs_per_step = PROJ_TN // ATTN_W
    return pl.pallas_call(
        _inproj_kernel,
        out_shape=(jax.ShapeDtypeStruct((SEQ, PL_WIDTH), F32),
                   jax.ShapeDtypeStruct((slabs, SEQ, ATTN_W), BF16),
                   jax.ShapeDtypeStruct((slabs, DIL_PLANES, SEQ // DIL_PLANES, ATTN_W), BF16)),
        grid=(SEQ // PROJ_TM, IN_WIDTH // PROJ_TN),
        in_specs=[
            pl.BlockSpec((PROJ_TM, D_MODEL), lambda i, n: (i, 0)),
            pl.BlockSpec((1, D_MODEL), lambda i, n: (0, 0)),
            pl.BlockSpec((None, D_MODEL, PROJ_TN), lambda i, n: (n, layer, 0)),
        ],
        out_specs=(
            pl.BlockSpec((PROJ_TM, PROJ_TN), lambda i, n: (i, jnp.minimum(n, PL_TILES - 1))),
            pl.BlockSpec((slabs_per_step, PROJ_TM, ATTN_W), lambda i, n: (qkv_col(n), i, 0)),
            pl.BlockSpec((slabs_per_step, DIL_PLANES, rows_per_plane, ATTN_W),
                         lambda i, n: (qkv_col(n), 0, i, 0)),
        ),
        scratch_shapes=[pltpu.VMEM((PROJ_TM, D_MODEL), BF16),
                        pltpu.VMEM((PROJ_TM, 1), F32),
                        pltpu.VMEM((PROJ_TN // V7X_LANES, PROJ_TM, V7X_LANES), F32),
                        pltpu.VMEM((PROJ_TN // V7X_LANES, DIL_STEP, PROJ_TM // DIL_STEP, V7X_LANES), F32)],
        compiler_params=pltpu.CompilerParams(
            dimension_semantics=("parallel", "arbitrary"),
            vmem_limit_bytes=_vmem_limit(vmem)),
        name="mix_inproj",
    )(x, g, w_in)


def _shift_rows(v, k, fill):
    t = v.shape[0]
    if k % 8 == 0:
        pad = jnp.full((k, v.shape[1]), fill, v.dtype)
        return jnp.concatenate([pad, v[: t - k]], axis=0)
    rolled = pltpu.roll(v, k, 0)
    row = lax.broadcasted_iota(jnp.int32, v.shape, 0)
    return jnp.where(row >= k, rolled, fill)


def _mix_kernel(z_ref, pw_ref, ps_ref, cw_ref, cb_ref, wr_ref, wi_ref, br_ref, bi_ref,
                lam_ref, y_ref, ext_ref, h_ref):
    i = pl.program_id(0)
    T = MIX_T
    H = MIX_HALO

    @pl.when(i == 0)
    def _():
        ext_ref[0:H, :] = jnp.zeros((H, POOL_WIDTH + LRU_WIDTH), F32)
        h_ref[...] = jnp.zeros_like(h_ref)

    ext_ref[H:H + T, :] = z_ref[:, 0:POOL_WIDTH + LRU_WIDTH]

    pos = (i * T + 1 + lax.broadcasted_iota(jnp.int32, (T, POOL_GROUP), 0)).astype(F32)
    for g, win in enumerate(POOL_WINDOWS):
        lanes = slice(g * POOL_GROUP, (g + 1) * POOL_GROUP)
        e = ext_ref[:, lanes]
        s = e
        k = 1
        while k < win:
            s = s + pltpu.roll(s, k, 0)
            k *= 2
        u = e[H:H + T]
        pooled = s[H:H + T] / jnp.minimum(pos, float(win)) - u
        y = jnp.dot(pooled.astype(BF16), pw_ref[g], preferred_element_type=F32)
        y_ref[:, lanes] = (y * ps_ref[:, lanes]).astype(BF16)

    xl = slice(POOL_WIDTH, POOL_WIDTH + LRU_WIDTH)
    xc = cb_ref[...] + cw_ref[LRU_CONV - 1:LRU_CONV, :] * ext_ref[H:H + T, xl]
    for k in range(1, LRU_CONV):
        xc = xc + cw_ref[LRU_CONV - 1 - k:LRU_CONV - k, :] * ext_ref[H - k:H - k + T, xl]
    xcb = xc.astype(BF16)
    r = jax.nn.sigmoid(jnp.dot(xcb, wr_ref[...], preferred_element_type=F32) + br_ref[...])
    ig = jax.nn.sigmoid(jnp.dot(xcb, wi_ref[...], preferred_element_type=F32) + bi_ref[...])
    nlam = -lam_ref[...]
    softplus = jnp.maximum(nlam, 0.0) + jnp.log1p(jnp.exp(-jnp.abs(nlam)))
    log_a = (-LRU_C * r) * softplus
    a = jnp.exp(log_a)
    b = jnp.sqrt(-jnp.tanh(log_a) * (a * a + 1.0)) * (ig * xc)
    k = 1
    while k < T:
        b = a * _shift_rows(b, k, 0.0) + b
        a = a * _shift_rows(a, k, 1.0)
        k *= 2
    h = a * h_ref[0:1, :] + b
    h_ref[0:1, :] = h[T - 1:T, :]
    gate_in = z_ref[:, POOL_WIDTH + LRU_WIDTH:PL_WIDTH]
    y_ref[:, POOL_WIDTH:POOL_WIDTH + LRU_WIDTH] = (h * jax.nn.gelu(gate_in)).astype(BF16)

    ext_ref[0:H, :] = ext_ref[T:T + H, :]


def _mix(zpl, pool_w, pool_scale, conv_w, conv_b, w_r, w_i, b_r, b_i, lam):
    full = lambda shape: pl.BlockSpec(shape, lambda i: (0,) * len(shape))
    vmem = (2 * MIX_T * PL_WIDTH * 4 + 2 * MIX_T * 1024 * 2 + (MIX_T + MIX_HALO) * 1024 * 4
            + 4 * 512 * 512 * 2 + 24 * MIX_T * LRU_WIDTH * 4)
    return pl.pallas_call(
        _mix_kernel,
        out_shape=jax.ShapeDtypeStruct((SEQ, POOL_WIDTH + LRU_WIDTH), BF16),
        grid=(SEQ // MIX_T,),
        in_specs=[
            pl.BlockSpec((MIX_T, PL_WIDTH), lambda i: (i, 0)),
            full((len(POOL_WINDOWS), POOL_GROUP, POOL_GROUP)),
            full((1, POOL_WIDTH)),
            full((LRU_CONV, LRU_WIDTH)),
            full((1, LRU_WIDTH)),
            full((LRU_WIDTH, LRU_WIDTH)),
            full((LRU_WIDTH, LRU_WIDTH)),
            full((1, LRU_WIDTH)),
            full((1, LRU_WIDTH)),
            full((1, LRU_WIDTH)),
        ],
        out_specs=pl.BlockSpec((MIX_T, POOL_WIDTH + LRU_WIDTH), lambda i: (i, 0)),
        scratch_shapes=[pltpu.VMEM((MIX_T + MIX_HALO, POOL_WIDTH + LRU_WIDTH), F32),
                        pltpu.VMEM((8, LRU_WIDTH), F32)],
        compiler_params=pltpu.CompilerParams(
            dimension_semantics=("arbitrary",),
            vmem_limit_bytes=_vmem_limit(vmem)),
        name="pool_lru",
    )(zpl, pool_w, pool_scale, conv_w, conv_b, w_r, w_i, b_r, b_i, lam)


_CONTRACT_LANES = (((1,), (1,)), ((), ()))
_CONTRACT_ROWS = (((0,), (0,)), ((), ()))


def _mod_pow2(x, n):
    assert n & (n - 1) == 0
    return jnp.bitwise_and(x, n - 1)


def _div_pow2(x, n):
    assert n & (n - 1) == 0
    return jnp.right_shift(x, n.bit_length() - 1)


def _build_bias(hg, bias_s):
    B = ATTN_BLOCK
    u = lax.broadcasted_iota(jnp.int32, (2 * B, 2 * B), 0)
    col = lax.broadcasted_iota(jnp.int32, (2 * B, 2 * B), 1)
    qi = _mod_pow2(col, B)
    heads_per_step = ATTN_W // ATTN_HEAD_DIM
    step_ratio = 2.0 ** (-8.0 * heads_per_step / ATTN_HEADS)
    group_scale = jnp.float32(1.0)
    for grp in range(1, ATTN_WIDTH // ATTN_W):
        group_scale = jnp.where(hg == grp, step_ratio ** grp, group_scale)
    for bi, dil in enumerate(ATTN_DILATIONS):
        if dil == 4:
            delta = (4 * (_mod_pow2(qi, QUARTER) - _mod_pow2(u, 2 * QUARTER) + QUARTER)
                     + _div_pow2(qi, QUARTER) - _div_pow2(u, 2 * QUARTER))
            prev = _mod_pow2(u, 2 * QUARTER) < QUARTER
        else:
            delta = qi - u + B
            prev = u < B
        valid = jnp.logical_and(delta >= 0, delta <= B)
        dist = (delta * dil).astype(F32)
        for hp in range(HEAD_PAIRS):
            slope_a = 2.0 ** (-8.0 * (2 * hp + 1) / ATTN_HEADS)
            slope_b = 2.0 ** (-8.0 * (2 * hp + 2) / ATTN_HEADS)
            slope = group_scale * jnp.where(col < B, slope_a, slope_b)
            bias = (-slope) * dist
            bias_s[bi, 0, hp] = jnp.where(valid, bias, NEG_INF)
            bias_s[bi, 1, hp] = jnp.where(jnp.logical_and(valid, jnp.logical_not(prev)), bias, NEG_INF)


def _scores_stage(q_pair, kcat, bias, s_ref, m_ref, slot):
    low = lax.broadcasted_iota(jnp.int32, q_pair.shape, 1) < ATTN_HEAD_DIM
    zero = jnp.zeros_like(q_pair)
    qcat = jnp.concatenate([jnp.where(low, q_pair, zero), jnp.where(low, zero, q_pair)], axis=0)
    s = lax.dot_general(kcat, qcat, _CONTRACT_LANES, preferred_element_type=F32) + bias
    s_ref[slot] = s
    m_ref[slot, 0:1, :] = jnp.max(s, axis=0, keepdims=True)


def _probs_stage(vcat, s_ref, m_ref, ot_ref, ml_ref, slot):
    m = m_ref[slot, 0:1, :]
    p = jnp.exp(s_ref[slot] - m)
    ot_ref[slot] = lax.dot_general(vcat, p.astype(BF16), _CONTRACT_ROWS, preferred_element_type=F32)
    ml_ref[slot, 0:1, :] = m
    ml_ref[slot, 1:2, :] = jnp.sum(p, axis=0, keepdims=True)


def _finish_stage(ot_ref, ml_ref, slot):
    B = ATTN_BLOCK
    H = ATTN_HEAD_DIM
    ot = ot_ref[slot]
    m = ml_ref[slot, 0:1, :]
    l = ml_ref[slot, 1:2, :]
    inv = 1.0 / l
    lse = m + jnp.log(l)
    o_t = jnp.concatenate([ot[0:H, 0:B] * inv[:, 0:B], ot[H:2 * H, B:2 * B] * inv[:, B:2 * B]], axis=0)
    l_t = jnp.concatenate([jnp.broadcast_to(lse[:, 0:B], (H, B)),
                           jnp.broadcast_to(lse[:, B:2 * B], (H, B))], axis=0)
    return o_t.T, l_t.T


def _software_pipeline(n, scores, probs, finish):
    scores(0)
    probs(0)
    scores(1)

    def body(t, carry):
        finish(t - 2)
        probs(t - 1)
        scores(t)
        return carry
    lax.fori_loop(2, n, body, 0)
    finish(n - 2)
    probs(n - 1)
    finish(n - 1)


def _attn_kernel(qn, kn, vn, q16, k16, v16, y_ref, kxn, vxn, kx16, vx16, o_s, l_s, bias_s, *stage):
    hg = pl.program_id(0)
    j = pl.program_id(1)
    B = ATTN_BLOCK
    T = ATTN_TILE
    first = jnp.where(j == 0, 1, 0)

    @pl.when(j == 0)
    def _():
        kxn[0:B, :] = jnp.zeros((B, ATTN_W), BF16)
        vxn[0:B, :] = jnp.zeros((B, ATTN_W), BF16)
        kx16[:, 0:B, :] = jnp.zeros((DIL_PLANES, B, ATTN_W), BF16)
        vx16[:, 0:B, :] = jnp.zeros((DIL_PLANES, B, ATTN_W), BF16)
        _build_bias(hg, bias_s)

    kxn[B:B + T, :] = kn[...]
    vxn[B:B + T, :] = vn[...]
    kx16[:, B:2 * B, :] = k16[...]
    vx16[:, B:2 * B, :] = v16[...]

    def lanes_of(hp):
        return slice(hp * V7X_LANES, (hp + 1) * V7X_LANES)

    def aligned(x, m):
        return x if isinstance(x, int) else pl.multiple_of(x, m)

    def run_branch(bi, fetch_q, fetch_k, fetch_v, variant_of, scatter):
        def units(t):
            for i in range(ATTN_GROUP):
                for hp in range(HEAD_PAIRS):
                    yield t * ATTN_GROUP + i, hp, stage[4 * (i * HEAD_PAIRS + hp):4 * (i * HEAD_PAIRS + hp) + 4]

        def scores(t):
            for blk, hp, (s_ref, m_ref, _, _) in units(t):
                _scores_stage(fetch_q(blk, hp), fetch_k(blk, hp), bias_s[bi, variant_of(blk), hp],
                              s_ref, m_ref, t % 2)

        def probs(t):
            for blk, hp, (s_ref, m_ref, ot_ref, ml_ref) in units(t):
                _probs_stage(fetch_v(blk, hp), s_ref, m_ref, ot_ref, ml_ref, t % 2)

        def finish(t):
            for blk, hp, (_, _, ot_ref, ml_ref) in units(t):
                o, l = _finish_stage(ot_ref, ml_ref, t % 2)
                scatter(blk, hp, o, l)

        _software_pipeline(DIL_PLANES // ATTN_GROUP, scores, probs, finish)

    def d1_rows(blk, n):
        return pl.ds(aligned(blk * B, B), n)

    def d1_scatter(blk, hp, o, l):
        o_s[0, hp, d1_rows(blk, B), :] = o
        l_s[0, hp, d1_rows(blk, B), :] = l

    run_branch(0,
               lambda blk, hp: qn[d1_rows(blk, B), lanes_of(hp)],
               lambda blk, hp: kxn[d1_rows(blk, 2 * B), lanes_of(hp)],
               lambda blk, hp: vxn[d1_rows(blk, 2 * B), lanes_of(hp)],
               lambda blk: jnp.where(blk == 0, first, 0),
               d1_scatter)

    per_plane = ATTN_BLOCK // QUARTER

    def d4_gather(ref, blk, hp, row0, n):
        c = blk // per_plane
        rows = pl.ds(aligned(row0 + (blk % per_plane) * QUARTER, QUARTER), n)
        return jnp.concatenate([ref[per_plane * a + c, rows, lanes_of(hp)] for a in range(per_plane)], axis=0)

    def d4_scatter(blk, hp, o, l):
        c = blk // per_plane
        b = blk % per_plane
        for a in range(per_plane):
            rows = pl.ds(b * QUARTER * DIL_PLANES + per_plane * a + c, QUARTER, stride=DIL_PLANES)
            o_s[1, hp, rows, :] = o[a * QUARTER:(a + 1) * QUARTER]
            l_s[1, hp, rows, :] = l[a * QUARTER:(a + 1) * QUARTER]

    run_branch(1,
               lambda blk, hp: d4_gather(q16, blk, hp, 0, QUARTER),
               lambda blk, hp: d4_gather(kx16, blk, hp, B - QUARTER, 2 * QUARTER),
               lambda blk, hp: d4_gather(vx16, blk, hp, B - QUARTER, 2 * QUARTER),
               lambda blk: jnp.where(blk % per_plane == 0, first, 0),
               d4_scatter)

    def d16_scatter(blk, hp, o, l):
        rows = pl.ds(blk, B, stride=DIL_PLANES)
        o_s[2, hp, rows, :] = o
        l_s[2, hp, rows, :] = l

    run_branch(2,
               lambda blk, hp: q16[blk, :, lanes_of(hp)],
               lambda blk, hp: kx16[blk, :, lanes_of(hp)],
               lambda blk, hp: vx16[blk, :, lanes_of(hp)],
               lambda blk: first,
               d16_scatter)

    def merge(c, carry):
        rows = pl.ds(pl.multiple_of(c * COMBINE_ROWS, COMBINE_ROWS), COMBINE_ROWS)
        for hp in range(HEAD_PAIRS):
            l1 = l_s[0, hp, rows, :]
            l4 = l_s[1, hp, rows, :]
            l16 = l_s[2, hp, rows, :]
            top = jnp.maximum(jnp.maximum(l1, l4), l16)
            w1 = jnp.exp(l1 - top)
            w4 = jnp.exp(l4 - top)
            w16 = jnp.exp(l16 - top)
            num = w1 * o_s[0, hp, rows, :] + w4 * o_s[1, hp, rows, :] + w16 * o_s[2, hp, rows, :]
            y_ref[rows, lanes_of(hp)] = (num / (w1 + w4 + w16)).astype(BF16)
        return carry
    lax.fori_loop(0, T // COMBINE_ROWS, merge, 0)

    kxn[0:B, :] = kxn[T:T + B, :]
    vxn[0:B, :] = vxn[T:T + B, :]
    kx16[:, 0:B, :] = kx16[:, B:2 * B, :]
    vx16[:, 0:B, :] = vx16[:, B:2 * B, :]


def _attention(qkv, qkv16):
    B = ATTN_BLOCK
    ngrp = ATTN_WIDTH // ATTN_W
    nat = lambda which: pl.BlockSpec((None, ATTN_TILE, ATTN_W), lambda hg, j: (which * ngrp + hg, j, 0))
    planes = lambda which: pl.BlockSpec((None, DIL_PLANES, B, ATTN_W),
                                        lambda hg, j: (which * ngrp + hg, 0, j, 0))
    vmem = (2 * 6 * ATTN_TILE * ATTN_W * 2 + 2 * ATTN_TILE * ATTN_W * 2
            + 2 * (ATTN_TILE + B) * ATTN_W * 2 + 2 * DIL_PLANES * 2 * B * ATTN_W * 2
            + 2 * 3 * HEAD_PAIRS * ATTN_TILE * V7X_LANES * 4
            + 3 * 2 * HEAD_PAIRS * 4 * B * B * 4
            + 16 * 1024 * 1024)
    return pl.pallas_call(
        _attn_kernel,
        out_shape=jax.ShapeDtypeStruct((ngrp, SEQ, ATTN_W), BF16),
        grid=(ngrp, SEQ // ATTN_TILE),
        in_specs=[nat(0), nat(1), nat(2), planes(0), planes(1), planes(2)],
        out_specs=pl.BlockSpec((None, ATTN_TILE, ATTN_W), lambda hg, j: (hg, j, 0)),
        scratch_shapes=[
            pltpu.VMEM((ATTN_TILE + B, ATTN_W), BF16),
            pltpu.VMEM((ATTN_TILE + B, ATTN_W), BF16),
            pltpu.VMEM((DIL_PLANES, 2 * B, ATTN_W), BF16),
            pltpu.VMEM((DIL_PLANES, 2 * B, ATTN_W), BF16),
            pltpu.VMEM((3, HEAD_PAIRS, ATTN_TILE, V7X_LANES), F32),
            pltpu.VMEM((3, HEAD_PAIRS, ATTN_TILE, V7X_LANES), F32),
            pltpu.VMEM((3, 2, HEAD_PAIRS, 2 * B, 2 * B), F32),
        ] + [
            pltpu.VMEM(shape, F32)
            for _ in range(ATTN_GROUP * HEAD_PAIRS)
            for shape in ((2, 2 * B, 2 * B), (2, 8, 2 * B), (2, B, 2 * B), (2, 8, 2 * B))
        ],
        compiler_params=pltpu.CompilerParams(
            dimension_semantics=("arbitrary", "arbitrary"),
            vmem_limit_bytes=_vmem_limit(vmem)),
        name="dilated_attn",
    )(qkv, qkv, qkv, qkv16, qkv16, qkv16)


def _outproj_kernel(x_ref, ypl_ref, ya_ref, g_ref, w_ref, o_ref):
    half = POOL_WIDTH + LRU_WIDTH

    def body(rows):
        h = jnp.dot(ypl_ref[rows, :], w_ref[0:half, :], preferred_element_type=F32)
        for grp in range(ya_ref.shape[0]):
            w_rows = slice(half + grp * ATTN_W, half + (grp + 1) * ATTN_W)
            h = h + jnp.dot(ya_ref[grp, rows, :], w_ref[w_rows, :], preferred_element_type=F32)
        o_ref[rows, :] = x_ref[rows, :] + (h * _rms_scale(h)) * g_ref[...]
    _for_row_chunks(o_ref.shape[0], body, OUT_ROWS)


def _outproj(x, ypl, ya, g, w_out, layer):
    half = POOL_WIDTH + LRU_WIDTH
    vmem = (4 * OUT_TM * D_MODEL * 4 + 4 * OUT_TM * half * 2 + 2 * D_MODEL * D_MODEL * 2
            + 4 * OUT_TM * D_MODEL * 4)
    return pl.pallas_call(
        _outproj_kernel,
        out_shape=jax.ShapeDtypeStruct((SEQ, D_MODEL), F32),
        grid=(SEQ // OUT_TM,),
        in_specs=[
            pl.BlockSpec((OUT_TM, D_MODEL), lambda i: (i, 0)),
            pl.BlockSpec((OUT_TM, half), lambda i: (i, 0)),
            pl.BlockSpec((ATTN_WIDTH // ATTN_W, OUT_TM, ATTN_W), lambda i: (0, i, 0)),
            pl.BlockSpec((1, D_MODEL), lambda i: (0, 0)),
            pl.BlockSpec((None, D_MODEL, D_MODEL), lambda i: (0, layer, 0)),
        ],
        out_specs=pl.BlockSpec((OUT_TM, D_MODEL), lambda i: (i, 0)),
        compiler_params=pltpu.CompilerParams(
            dimension_semantics=("parallel",),
            vmem_limit_bytes=_vmem_limit(vmem)),
        name="mix_outproj",
    )(x, ypl, ya, g, w_out)


def _store_col_blocks(vals, o_ref):
    cb = o_ref.shape[2]
    for b in range(o_ref.shape[0]):
        o_ref[b] = vals[:, b * cb:(b + 1) * cb]


def _cast_kernel(w_ref, o_ref):
    _store_col_blocks(w_ref[...].astype(BF16), o_ref)


def _scaled_cast_kernel(w_ref, s_ref, o_ref):
    _store_col_blocks((w_ref[...] * s_ref[...]).astype(BF16), o_ref)


def _to_bf16(w, col_block, col_scale=None):
    depth, r, c = w.shape
    rows = depth * r
    tr = rows
    while tr * c * 4 > CAST_BLOCK_BYTES:
        tr //= 2
    in_specs = [pl.BlockSpec((tr, c), lambda i: (i, 0))]
    args = [w.reshape(rows, c)]
    body = _cast_kernel
    if col_scale is not None:
        in_specs.append(pl.BlockSpec((1, c), lambda i: (0, 0)))
        args.append(col_scale.reshape(1, c))
        body = _scaled_cast_kernel
    return pl.pallas_call(
        body,
        out_shape=jax.ShapeDtypeStruct((c // col_block, rows, col_block), BF16),
        grid=(rows // tr,),
        in_specs=in_specs,
        out_specs=pl.BlockSpec((c // col_block, tr, col_block), lambda i: (0, i, 0)),
        compiler_params=pltpu.CompilerParams(
            dimension_semantics=("parallel",),
            vmem_limit_bytes=_vmem_limit(2 * tr * c * (4 + 2) + 2 * tr * c * 4)),
        name="cast_bf16",
    )(*args)


def _block_diag(w):
    heads, c, _ = w.shape
    eye = jnp.eye(heads, dtype=w.dtype)
    return (eye[:, None, :, None] * w[:, :, None, :]).reshape(heads * c, heads * c)


def kernel(x, norm_g, ffn1_w_in, ffn1_w_out, mix_w_in, mix_w_out, pool_w, pool_scale,
           lru_conv_w, lru_conv_b, lru_gate_w, lru_gate_b, lru_lambda, ffn2_w_in, ffn2_w_out):
    xs = x.reshape(SEQ, D_MODEL)
    w1_in = _to_bf16(ffn1_w_in, FFN_TF)
    w1_out = _to_bf16(ffn1_w_out, D_MODEL)
    w2_in = _to_bf16(ffn2_w_in, FFN_TF)
    w2_out = _to_bf16(ffn2_w_out, D_MODEL)
    qscale = jnp.concatenate([jnp.ones((PL_WIDTH,), F32),
                              jnp.full((ATTN_WIDTH,), ATTN_HEAD_DIM ** -0.5, F32),
                              jnp.ones((2 * ATTN_WIDTH,), F32)])
    wm_in = _to_bf16(mix_w_in, PROJ_TN, qscale)
    wm_out = _to_bf16(mix_w_out, D_MODEL)
    pw = pool_w.astype(BF16)
    for layer in range(DEPTH):
        g = norm_g[layer].reshape(6, 1, D_MODEL)
        xs = _ffn(xs, g[0], g[1], w1_in, w1_out, layer)
        zpl, qkv, qkv16 = _inproj(xs, g[2], wm_in, layer)
        ypl = _mix(zpl, pw[layer], pool_scale[layer].reshape(1, POOL_WIDTH),
                   lru_conv_w[layer], lru_conv_b[layer].reshape(1, LRU_WIDTH),
                   _block_diag(lru_gate_w[layer, 0]).astype(BF16),
                   _block_diag(lru_gate_w[layer, 1]).astype(BF16),
                   lru_gate_b[layer, 0].reshape(1, LRU_WIDTH),
                   lru_gate_b[layer, 1].reshape(1, LRU_WIDTH),
                   lru_lambda[layer].reshape(1, LRU_WIDTH))
        ya = _attention(qkv, qkv16)
        xs = _outproj(xs, ypl, ya, g[3], wm_out, layer)
        xs = _ffn(xs, g[4], g[5], w2_in, w2_out, layer)
    return xs.reshape(x.shape)
```

```python
import jax
import jax.numpy as jnp
from jax import lax
from jax.experimental import pallas as pl
from jax.experimental.pallas import tpu as pltpu

F32 = jnp.float32
BF16 = jnp.bfloat16

D_MODEL = 2048
SEQ = 16384
DEPTH = 2
D_FF = 5632
POOL_WIDTH = 512
POOL_WINDOWS = (2, 4, 8, 16)
POOL_GROUP = 128
LRU_WIDTH = 512
LRU_HEADS = 8
LRU_HEAD_DIM = 64
LRU_CONV = 4
LRU_C = 8.0
ATTN_WIDTH = 1024
ATTN_HEAD_DIM = 64
ATTN_HEADS = 16
ATTN_DILATIONS = (1, 4, 16)
ATTN_BLOCK = 128
PL_WIDTH = POOL_WIDTH + 2 * LRU_WIDTH
IN_WIDTH = PL_WIDTH + 3 * ATTN_WIDTH
NORM_EPS = 1e-6
NEG_INF = -1e30

V7X_LANES = 128
VMEM_LIMIT_CAP = 60000 * 1024

FFN_TM = 1024
FFN_TF = 512
FFN_EDGE_ROWS = 256
PROJ_TM = 1024
PROJ_TN = 768
OUT_TM = 512
OUT_ROWS = 256
MIX_T = 512
MIX_HALO = 16
DIL_PLANES = 16
DIL_STEP = 4
ATTN_TILE = DIL_PLANES * ATTN_BLOCK
ATTN_W = 256
HEAD_PAIRS = ATTN_W // V7X_LANES
ATTN_GROUP = 4
QUARTER = ATTN_BLOCK // 4
COMBINE_ROWS = 256
CAST_BLOCK_BYTES = 8 * 1024 * 1024


def _vmem_limit(nbytes):
    return int(min(VMEM_LIMIT_CAP, nbytes))


def _rms_scale(x):
    return lax.rsqrt(jnp.mean(x * x, axis=-1, keepdims=True) + NORM_EPS)


def _ffn_kernel(x_ref, gin_ref, gout_ref, wg_ref, wu_ref, wo_ref, o_ref, xn_ref):
    f = pl.program_id(1)
    last = pl.num_programs(1) - 1

    def swiglu_part(rows):
        xn = xn_ref[rows, :]
        gate = jnp.dot(xn, wg_ref[...], preferred_element_type=F32)
        up = jnp.dot(xn, wu_ref[...], preferred_element_type=F32)
        h = ((gate * jax.nn.sigmoid(gate)) * up).astype(BF16)
        return jnp.dot(h, wo_ref[...], preferred_element_type=F32)

    edge_chunks = [slice(c * FFN_EDGE_ROWS, (c + 1) * FFN_EDGE_ROWS)
                   for c in range(o_ref.shape[0] // FFN_EDGE_ROWS)]

    @pl.when(f == 0)
    def _():
        for rows in edge_chunks:
            scale = _rms_scale(x_ref[rows, :])
            xn_ref[rows, :] = ((x_ref[rows, :] * scale) * gin_ref[...]).astype(BF16)
            o_ref[rows, :] = swiglu_part(rows)

    @pl.when(jnp.logical_and(f > 0, f < last))
    def _():
        o_ref[...] += swiglu_part(slice(None))

    @pl.when(f == last)
    def _():
        half_g = 0.5 * gout_ref[...]
        for rows in edge_chunks:
            o_ref[rows, :] += swiglu_part(rows)
            scale = _rms_scale(o_ref[rows, :])
            o_ref[rows, :] = x_ref[rows, :] + (o_ref[rows, :] * scale) * half_g


def _ffn(x, g_in, g_out, w_in, w_out, layer):
    nf = D_FF // FFN_TF
    vmem = (4 * FFN_TM * D_MODEL * 4
            + FFN_TM * D_MODEL * 2
            + 2 * 3 * D_MODEL * FFN_TF * 2
            + 8 * FFN_TM * FFN_TF * 4)
    return pl.pallas_call(
        _ffn_kernel,
        out_shape=jax.ShapeDtypeStruct((SEQ, D_MODEL), F32),
        grid=(SEQ // FFN_TM, nf),
        in_specs=[
            pl.BlockSpec((FFN_TM, D_MODEL), lambda i, f: (i, 0)),
            pl.BlockSpec((1, D_MODEL), lambda i, f: (0, 0)),
            pl.BlockSpec((1, D_MODEL), lambda i, f: (0, 0)),
            pl.BlockSpec((None, D_MODEL, FFN_TF), lambda i, f: (f, layer, 0)),
            pl.BlockSpec((None, D_MODEL, FFN_TF), lambda i, f: (f + nf, layer, 0)),
            pl.BlockSpec((None, FFN_TF, D_MODEL), lambda i, f: (0, layer * nf + f, 0)),
        ],
        out_specs=pl.BlockSpec((FFN_TM, D_MODEL), lambda i, f: (i, 0)),
        scratch_shapes=[pltpu.VMEM((FFN_TM, D_MODEL), BF16)],
        compiler_params=pltpu.CompilerParams(
            dimension_semantics=("parallel", "arbitrary"),
            vmem_limit_bytes=_vmem_limit(vmem)),
        name="ffn",
    )(x, g_in, g_out, w_in, w_in, w_out)


PL_TILES = PL_WIDTH // PROJ_TN


def _inproj_kernel(x_ref, g_ref, w_ref, zpl_ref, qkv_ref, qkv16_ref, xn_ref, zs_ref, zt_ref):
    n = pl.program_id(1)

    @pl.when(n == 0)
    def _():
        for c in range(PROJ_TM // FFN_EDGE_ROWS):
            rows = slice(c * FFN_EDGE_ROWS, (c + 1) * FFN_EDGE_ROWS)
            scale = _rms_scale(x_ref[rows, :])
            xn_ref[rows, :] = ((x_ref[rows, :] * scale) * g_ref[...]).astype(BF16)
            zpl_ref[rows, :] = jnp.dot(xn_ref[rows, :], w_ref[...], preferred_element_type=F32)

    @pl.when(jnp.logical_and(n > 0, n < PL_TILES))
    def _():
        zpl_ref[...] = jnp.dot(xn_ref[...], w_ref[...], preferred_element_type=F32)

    @pl.when(n >= PL_TILES)
    def _():
        z = jnp.dot(xn_ref[...], w_ref[...], preferred_element_type=F32)
        _store_col_blocks(z.astype(BF16), qkv_ref)
        per_slab = ATTN_W // V7X_LANES
        for c in range(PROJ_TN // V7X_LANES):
            zs_ref[c] = z[:, c * V7X_LANES:(c + 1) * V7X_LANES]
            for r1 in range(DIL_STEP):
                zt_ref[c, r1] = zs_ref[c, pl.ds(r1, PROJ_TM // DIL_STEP, stride=DIL_STEP), :]
            lanes = slice((c % per_slab) * V7X_LANES, (c % per_slab + 1) * V7X_LANES)
            for r1 in range(DIL_STEP):
                for r2 in range(DIL_STEP):
                    plane = zt_ref[c, r1, pl.ds(r2, PROJ_TM // DIL_PLANES, stride=DIL_STEP), :]
                    qkv16_ref[c // per_slab, r1 + DIL_STEP * r2, :, lanes] = plane.astype(BF16)


def _inproj(x, g, w_in, layer):
    rows_per_plane = PROJ_TM // DIL_PLANES
    vmem = (2 * PROJ_TM * D_MODEL * 4 + PROJ_TM * D_MODEL * 2 + 2 * D_MODEL * PROJ_TN * 2
            + 2 * PROJ_TM * PROJ_TN * (4 + 2 + 2) + 4 * PROJ_TM * PROJ_TN * 4)
    qkv_col = lambda n: jnp.maximum(n - PL_TILES, 0)
    slabs = 3 * ATTN_WIDTH // ATTN_W
    slabs_per_step = PROJ_TN // ATTN_W
    return pl.pallas_call(
        _inproj_kernel,
        out_shape=(jax.ShapeDtypeStruct((SEQ, PL_WIDTH), F32),
                   jax.ShapeDtypeStruct((slabs, SEQ, ATTN_W), BF16),
                   jax.ShapeDtypeStruct((slabs, DIL_PLANES, SEQ // DIL_PLANES, ATTN_W), BF16)),
        grid=(SEQ // PROJ_TM, IN_WIDTH // PROJ_TN),
        in_specs=[
            pl.BlockSpec((PROJ_TM, D_MODEL), lambda i, n: (i, 0)),
            pl.BlockSpec((1, D_MODEL), lambda i, n: (0, 0)),
            pl.BlockSpec((None, D_MODEL, PROJ_TN), lambda i, n: (n, layer, 0)),
        ],
        out_specs=(
            pl.BlockSpec((PROJ_TM, PROJ_TN), lambda i, n: (i, jnp.minimum(n, PL_TILES - 1))),
            pl.BlockSpec((slabs_per_step, PROJ_TM, ATTN_W), lambda i, n: (qkv_col(n), i, 0)),
            pl.BlockSpec((slabs_per_step, DIL_PLANES, rows_per_plane, ATTN_W),
                         lambda i, n: (qkv_col(n), 0, i, 0)),
        ),
        scratch_shapes=[pltpu.VMEM((PROJ_TM, D_MODEL), BF16),
                        pltpu.VMEM((PROJ_TN // V7X_LANES, PROJ_TM, V7X_LANES), F32),
                        pltpu.VMEM((PROJ_TN // V7X_LANES, DIL_STEP, PROJ_TM // DIL_STEP, V7X_LANES), F32)],
        compiler_params=pltpu.CompilerParams(
            dimension_semantics=("parallel", "arbitrary"),
            vmem_limit_bytes=_vmem_limit(vmem)),
        name="mix_inproj",
    )(x, g, w_in)


def _shift_rows(v, k, fill):
    t = v.shape[0]
    if k % 8 == 0:
        pad = jnp.full((k, v.shape[1]), fill, v.dtype)
        return jnp.concatenate([pad, v[: t - k]], axis=0)
    rolled = pltpu.roll(v, k, 0)
    row = lax.broadcasted_iota(jnp.int32, v.shape, 0)
    return jnp.where(row >= k, rolled, fill)


def _mix_kernel(z_ref, pw_ref, ps_ref, cw_ref, cb_ref, wr_ref, wi_ref, br_ref, bi_ref,
                lam_ref, y_ref, ext_ref, h_ref):
    i = pl.program_id(0)
    T = MIX_T
    H = MIX_HALO

    @pl.when(i == 0)
    def _():
        ext_ref[0:H, :] = jnp.zeros((H, POOL_WIDTH + LRU_WIDTH), F32)
        h_ref[...] = jnp.zeros_like(h_ref)

    ext_ref[H:H + T, :] = z_ref[:, 0:POOL_WIDTH + LRU_WIDTH]

    pos = (i * T + 1 + lax.broadcasted_iota(jnp.int32, (T, POOL_GROUP), 0)).astype(F32)
    for g, win in enumerate(POOL_WINDOWS):
        lanes = slice(g * POOL_GROUP, (g + 1) * POOL_GROUP)
        e = ext_ref[:, lanes]
        s = e
        k = 1
        while k < win:
            s = s + pltpu.roll(s, k, 0)
            k *= 2
        u = e[H:H + T]
        pooled = s[H:H + T] / jnp.minimum(pos, float(win)) - u
        y = jnp.dot(pooled.astype(BF16), pw_ref[g], preferred_element_type=F32)
        y_ref[:, lanes] = (y * ps_ref[:, lanes]).astype(BF16)

    xl = slice(POOL_WIDTH, POOL_WIDTH + LRU_WIDTH)
    xc = cb_ref[...] + cw_ref[LRU_CONV - 1:LRU_CONV, :] * ext_ref[H:H + T, xl]
    for k in range(1, LRU_CONV):
        xc = xc + cw_ref[LRU_CONV - 1 - k:LRU_CONV - k, :] * ext_ref[H - k:H - k + T, xl]
    xcb = xc.astype(BF16)
    r = jax.nn.sigmoid(jnp.dot(xcb, wr_ref[...], preferred_element_type=F32) + br_ref[...])
    ig = jax.nn.sigmoid(jnp.dot(xcb, wi_ref[...], preferred_element_type=F32) + bi_ref[...])
    nlam = -lam_ref[...]
    softplus = jnp.maximum(nlam, 0.0) + jnp.log1p(jnp.exp(-jnp.abs(nlam)))
    log_a = (-LRU_C * r) * softplus
    a = jnp.exp(log_a)
    b = jnp.sqrt(-jnp.tanh(log_a) * (a * a + 1.0)) * (ig * xc)
    k = 1
    while k < T:
        b = a * _shift_rows(b, k, 0.0) + b
        a = a * _shift_rows(a, k, 1.0)
        k *= 2
    h = a * h_ref[0:1, :] + b
    h_ref[0:1, :] = h[T - 1:T, :]
    gate_in = z_ref[:, POOL_WIDTH + LRU_WIDTH:PL_WIDTH]
    y_ref[:, POOL_WIDTH:POOL_WIDTH + LRU_WIDTH] = (h * jax.nn.gelu(gate_in)).astype(BF16)

    ext_ref[0:H, :] = ext_ref[T:T + H, :]


def _mix(zpl, pool_w, pool_scale, conv_w, conv_b, w_r, w_i, b_r, b_i, lam):
    full = lambda shape: pl.BlockSpec(shape, lambda i: (0,) * len(shape))
    vmem = (2 * MIX_T * PL_WIDTH * 4 + 2 * MIX_T * 1024 * 2 + (MIX_T + MIX_HALO) * 1024 * 4
            + 4 * 512 * 512 * 2 + 24 * MIX_T * LRU_WIDTH * 4)
    return pl.pallas_call(
        _mix_kernel,
        out_shape=jax.ShapeDtypeStruct((SEQ, POOL_WIDTH + LRU_WIDTH), BF16),
        grid=(SEQ // MIX_T,),
        in_specs=[
            pl.BlockSpec((MIX_T, PL_WIDTH), lambda i: (i, 0)),
            full((len(POOL_WINDOWS), POOL_GROUP, POOL_GROUP)),
            full((1, POOL_WIDTH)),
            full((LRU_CONV, LRU_WIDTH)),
            full((1, LRU_WIDTH)),
            full((LRU_WIDTH, LRU_WIDTH)),
            full((LRU_WIDTH, LRU_WIDTH)),
            full((1, LRU_WIDTH)),
            full((1, LRU_WIDTH)),
            full((1, LRU_WIDTH)),
        ],
        out_specs=pl.BlockSpec((MIX_T, POOL_WIDTH + LRU_WIDTH), lambda i: (i, 0)),
        scratch_shapes=[pltpu.VMEM((MIX_T + MIX_HALO, POOL_WIDTH + LRU_WIDTH), F32),
                        pltpu.VMEM((8, LRU_WIDTH), F32)],
        compiler_params=pltpu.CompilerParams(
            dimension_semantics=("arbitrary",),
            vmem_limit_bytes=_vmem_limit(vmem)),
        name="pool_lru",
    )(zpl, pool_w, pool_scale, conv_w, conv_b, w_r, w_i, b_r, b_i, lam)


_CONTRACT_LANES = (((1,), (1,)), ((), ()))
_CONTRACT_ROWS = (((0,), (0,)), ((), ()))


def _mod_pow2(x, n):
    assert n & (n - 1) == 0
    return jnp.bitwise_and(x, n - 1)


def _div_pow2(x, n):
    assert n & (n - 1) == 0
    return jnp.right_shift(x, n.bit_length() - 1)


def _build_bias(hg, bias_s):
    B = ATTN_BLOCK
    u = lax.broadcasted_iota(jnp.int32, (2 * B, 2 * B), 0)
    col = lax.broadcasted_iota(jnp.int32, (2 * B, 2 * B), 1)
    qi = _mod_pow2(col, B)
    heads_per_step = ATTN_W // ATTN_HEAD_DIM
    step_ratio = 2.0 ** (-8.0 * heads_per_step / ATTN_HEADS)
    group_scale = jnp.float32(1.0)
    for grp in range(1, ATTN_WIDTH // ATTN_W):
        group_scale = jnp.where(hg == grp, step_ratio ** grp, group_scale)
    for bi, dil in enumerate(ATTN_DILATIONS):
        if dil == 4:
            delta = (4 * (_mod_pow2(qi, QUARTER) - _mod_pow2(u, 2 * QUARTER) + QUARTER)
                     + _div_pow2(qi, QUARTER) - _div_pow2(u, 2 * QUARTER))
            prev = _mod_pow2(u, 2 * QUARTER) < QUARTER
        else:
            delta = qi - u + B
            prev = u < B
        valid = jnp.logical_and(delta >= 0, delta <= B)
        dist = (delta * dil).astype(F32)
        for hp in range(HEAD_PAIRS):
            slope_a = 2.0 ** (-8.0 * (2 * hp + 1) / ATTN_HEADS)
            slope_b = 2.0 ** (-8.0 * (2 * hp + 2) / ATTN_HEADS)
            slope = group_scale * jnp.where(col < B, slope_a, slope_b)
            bias = (-slope) * dist
            bias_s[bi, 0, hp] = jnp.where(valid, bias, NEG_INF)
            bias_s[bi, 1, hp] = jnp.where(jnp.logical_and(valid, jnp.logical_not(prev)), bias, NEG_INF)


def _scores_stage(q_pair, kcat, bias, s_ref, m_ref, slot):
    low = lax.broadcasted_iota(jnp.int32, q_pair.shape, 1) < ATTN_HEAD_DIM
    zero = jnp.zeros_like(q_pair)
    qcat = jnp.concatenate([jnp.where(low, q_pair, zero), jnp.where(low, zero, q_pair)], axis=0)
    s = lax.dot_general(kcat, qcat, _CONTRACT_LANES, preferred_element_type=F32) + bias
    s_ref[slot] = s
    m_ref[slot, 0:1, :] = jnp.max(s, axis=0, keepdims=True)


def _probs_stage(vcat, s_ref, m_ref, ot_ref, ml_ref, slot):
    m = m_ref[slot, 0:1, :]
    p = jnp.exp(s_ref[slot] - m)
    ot_ref[slot] = lax.dot_general(vcat, p.astype(BF16), _CONTRACT_ROWS, preferred_element_type=F32)
    ml_ref[slot, 0:1, :] = m
    ml_ref[slot, 1:2, :] = jnp.sum(p, axis=0, keepdims=True)


def _finish_stage(ot_ref, ml_ref, slot):
    B = ATTN_BLOCK
    H = ATTN_HEAD_DIM
    ot = ot_ref[slot]
    m = ml_ref[slot, 0:1, :]
    l = ml_ref[slot, 1:2, :]
    inv = 1.0 / l
    lse = m + jnp.log(l)
    o_t = jnp.concatenate([ot[0:H, 0:B] * inv[:, 0:B], ot[H:2 * H, B:2 * B] * inv[:, B:2 * B]], axis=0)
    l_t = jnp.concatenate([jnp.broadcast_to(lse[:, 0:B], (H, B)),
                           jnp.broadcast_to(lse[:, B:2 * B], (H, B))], axis=0)
    return o_t.T, l_t.T


def _software_pipeline(n, scores, probs, finish):
    scores(0)
    probs(0)
    scores(1)

    def body(t, carry):
        finish(t - 2)
        probs(t - 1)
        scores(t)
        return carry
    lax.fori_loop(2, n, body, 0)
    finish(n - 2)
    probs(n - 1)
    finish(n - 1)


def _attn_kernel(qn, kn, vn, q16, k16, v16, y_ref, kxn, vxn, kx16, vx16, o_s, l_s, bias_s, *stage):
    hg = pl.program_id(0)
    j = pl.program_id(1)
    B = ATTN_BLOCK
    T = ATTN_TILE
    first = jnp.where(j == 0, 1, 0)

    @pl.when(j == 0)
    def _():
        kxn[0:B, :] = jnp.zeros((B, ATTN_W), BF16)
        vxn[0:B, :] = jnp.zeros((B, ATTN_W), BF16)
        kx16[:, 0:B, :] = jnp.zeros((DIL_PLANES, B, ATTN_W), BF16)
        vx16[:, 0:B, :] = jnp.zeros((DIL_PLANES, B, ATTN_W), BF16)
        _build_bias(hg, bias_s)

    kxn[B:B + T, :] = kn[...]
    vxn[B:B + T, :] = vn[...]
    kx16[:, B:2 * B, :] = k16[...]
    vx16[:, B:2 * B, :] = v16[...]

    def lanes_of(hp):
        return slice(hp * V7X_LANES, (hp + 1) * V7X_LANES)

    def aligned(x, m):
        return x if isinstance(x, int) else pl.multiple_of(x, m)

    def run_branch(bi, fetch_q, fetch_k, fetch_v, variant_of, scatter):
        def units(t):
            for i in range(ATTN_GROUP):
                for hp in range(HEAD_PAIRS):
                    yield t * ATTN_GROUP + i, hp, stage[4 * (i * HEAD_PAIRS + hp):4 * (i * HEAD_PAIRS + hp) + 4]

        def scores(t):
            for blk, hp, (s_ref, m_ref, _, _) in units(t):
                _scores_stage(fetch_q(blk, hp), fetch_k(blk, hp), bias_s[bi, variant_of(blk), hp],
                              s_ref, m_ref, t % 2)

        def probs(t):
            for blk, hp, (s_ref, m_ref, ot_ref, ml_ref) in units(t):
                _probs_stage(fetch_v(blk, hp), s_ref, m_ref, ot_ref, ml_ref, t % 2)

        def finish(t):
            for blk, hp, (_, _, ot_ref, ml_ref) in units(t):
                o, l = _finish_stage(ot_ref, ml_ref, t % 2)
                scatter(blk, hp, o, l)

        _software_pipeline(DIL_PLANES // ATTN_GROUP, scores, probs, finish)

    def d1_rows(blk, n):
        return pl.ds(aligned(blk * B, B), n)

    def d1_scatter(blk, hp, o, l):
        o_s[0, hp, d1_rows(blk, B), :] = o
        l_s[0, hp, d1_rows(blk, B), :] = l

    run_branch(0,
               lambda blk, hp: qn[d1_rows(blk, B), lanes_of(hp)],
               lambda blk, hp: kxn[d1_rows(blk, 2 * B), lanes_of(hp)],
               lambda blk, hp: vxn[d1_rows(blk, 2 * B), lanes_of(hp)],
               lambda blk: jnp.where(blk == 0, first, 0),
               d1_scatter)

    per_plane = ATTN_BLOCK // QUARTER

    def d4_gather(ref, blk, hp, row0, n):
        c = blk // per_plane
        rows = pl.ds(aligned(row0 + (blk % per_plane) * QUARTER, QUARTER), n)
        return jnp.concatenate([ref[per_plane * a + c, rows, lanes_of(hp)] for a in range(per_plane)], axis=0)

    def d4_scatter(blk, hp, o, l):
        c = blk // per_plane
        b = blk % per_plane
        for a in range(per_plane):
            rows = pl.ds(b * QUARTER * DIL_PLANES + per_plane * a + c, QUARTER, stride=DIL_PLANES)
            o_s[1, hp, rows, :] = o[a * QUARTER:(a + 1) * QUARTER]
            l_s[1, hp, rows, :] = l[a * QUARTER:(a + 1) * QUARTER]

    run_branch(1,
               lambda blk, hp: d4_gather(q16, blk, hp, 0, QUARTER),
               lambda blk, hp: d4_gather(kx16, blk, hp, B - QUARTER, 2 * QUARTER),
               lambda blk, hp: d4_gather(vx16, blk, hp, B - QUARTER, 2 * QUARTER),
               lambda blk: jnp.where(blk % per_plane == 0, first, 0),
               d4_scatter)

    def d16_scatter(blk, hp, o, l):
        rows = pl.ds(blk, B, stride=DIL_PLANES)
        o_s[2, hp, rows, :] = o
        l_s[2, hp, rows, :] = l

    run_branch(2,
               lambda blk, hp: q16[blk, :, lanes_of(hp)],
               lambda blk, hp: kx16[blk, :, lanes_of(hp)],
               lambda blk, hp: vx16[blk, :, lanes_of(hp)],
               lambda blk: first,
               d16_scatter)

    def merge(c, carry):
        rows = pl.ds(pl.multiple_of(c * COMBINE_ROWS, COMBINE_ROWS), COMBINE_ROWS)
        for hp in range(HEAD_PAIRS):
            l1 = l_s[0, hp, rows, :]
            l4 = l_s[1, hp, rows, :]
            l16 = l_s[2, hp, rows, :]
            top = jnp.maximum(jnp.maximum(l1, l4), l16)
            w1 = jnp.exp(l1 - top)
            w4 = jnp.exp(l4 - top)
            w16 = jnp.exp(l16 - top)
            num = w1 * o_s[0, hp, rows, :] + w4 * o_s[1, hp, rows, :] + w16 * o_s[2, hp, rows, :]
            y_ref[rows, lanes_of(hp)] = (num / (w1 + w4 + w16)).astype(BF16)
        return carry
    lax.fori_loop(0, T // COMBINE_ROWS, merge, 0)

    kxn[0:B, :] = kxn[T:T + B, :]
    vxn[0:B, :] = vxn[T:T + B, :]
    kx16[:, 0:B, :] = kx16[:, B:2 * B, :]
    vx16[:, 0:B, :] = vx16[:, B:2 * B, :]


def _attention(qkv, qkv16):
    B = ATTN_BLOCK
    ngrp = ATTN_WIDTH // ATTN_W
    nat = lambda which: pl.BlockSpec((None, ATTN_TILE, ATTN_W), lambda hg, j: (which * ngrp + hg, j, 0))
    planes = lambda which: pl.BlockSpec((None, DIL_PLANES, B, ATTN_W),
                                        lambda hg, j: (which * ngrp + hg, 0, j, 0))
    vmem = (2 * 6 * ATTN_TILE * ATTN_W * 2 + 2 * ATTN_TILE * ATTN_W * 2
            + 2 * (ATTN_TILE + B) * ATTN_W * 2 + 2 * DIL_PLANES * 2 * B * ATTN_W * 2
            + 2 * 3 * HEAD_PAIRS * ATTN_TILE * V7X_LANES * 4
            + 3 * 2 * HEAD_PAIRS * 4 * B * B * 4
            + 16 * 1024 * 1024)
    return pl.pallas_call(
        _attn_kernel,
        out_shape=jax.ShapeDtypeStruct((ngrp, SEQ, ATTN_W), BF16),
        grid=(ngrp, SEQ // ATTN_TILE),
        in_specs=[nat(0), nat(1), nat(2), planes(0), planes(1), planes(2)],
        out_specs=pl.BlockSpec((None, ATTN_TILE, ATTN_W), lambda hg, j: (hg, j, 0)),
        scratch_shapes=[
            pltpu.VMEM((ATTN_TILE + B, ATTN_W), BF16),
            pltpu.VMEM((ATTN_TILE + B, ATTN_W), BF16),
            pltpu.VMEM((DIL_PLANES, 2 * B, ATTN_W), BF16),
            pltpu.VMEM((DIL_PLANES, 2 * B, ATTN_W), BF16),
            pltpu.VMEM((3, HEAD_PAIRS, ATTN_TILE, V7X_LANES), F32),
            pltpu.VMEM((3, HEAD_PAIRS, ATTN_TILE, V7X_LANES), F32),
            pltpu.VMEM((3, 2, HEAD_PAIRS, 2 * B, 2 * B), F32),
        ] + [
            pltpu.VMEM(shape, F32)
            for _ in range(ATTN_GROUP * HEAD_PAIRS)
            for shape in ((2, 2 * B, 2 * B), (2, 8, 2 * B), (2, B, 2 * B), (2, 8, 2 * B))
        ],
        compiler_params=pltpu.CompilerParams(
            dimension_semantics=("arbitrary", "arbitrary"),
            vmem_limit_bytes=_vmem_limit(vmem)),
        name="dilated_attn",
    )(qkv, qkv, qkv, qkv16, qkv16, qkv16)


def _outproj_kernel(x_ref, ypl_ref, ya_ref, g_ref, w_ref, o_ref):
    half = POOL_WIDTH + LRU_WIDTH

    for c in range(o_ref.shape[0] // OUT_ROWS):
        rows = slice(c * OUT_ROWS, (c + 1) * OUT_ROWS)
        h = jnp.dot(ypl_ref[rows, :], w_ref[0:half, :], preferred_element_type=F32)
        for grp in range(ya_ref.shape[0]):
            w_rows = slice(half + grp * ATTN_W, half + (grp + 1) * ATTN_W)
            h = h + jnp.dot(ya_ref[grp, rows, :], w_ref[w_rows, :], preferred_element_type=F32)
        o_ref[rows, :] = x_ref[rows, :] + (h * _rms_scale(h)) * g_ref[...]


def _outproj(x, ypl, ya, g, w_out, layer):
    half = POOL_WIDTH + LRU_WIDTH
    vmem = (4 * OUT_TM * D_MODEL * 4 + 4 * OUT_TM * half * 2 + 2 * D_MODEL * D_MODEL * 2
            + 4 * OUT_TM * D_MODEL * 4)
    return pl.pallas_call(
        _outproj_kernel,
        out_shape=jax.ShapeDtypeStruct((SEQ, D_MODEL), F32),
        grid=(SEQ // OUT_TM,),
        in_specs=[
            pl.BlockSpec((OUT_TM, D_MODEL), lambda i: (i, 0)),
            pl.BlockSpec((OUT_TM, half), lambda i: (i, 0)),
            pl.BlockSpec((ATTN_WIDTH // ATTN_W, OUT_TM, ATTN_W), lambda i: (0, i, 0)),
            pl.BlockSpec((1, D_MODEL), lambda i: (0, 0)),
            pl.BlockSpec((None, D_MODEL, D_MODEL), lambda i: (0, layer, 0)),
        ],
        out_specs=pl.BlockSpec((OUT_TM, D_MODEL), lambda i: (i, 0)),
        compiler_params=pltpu.CompilerParams(
            dimension_semantics=("parallel",),
            vmem_limit_bytes=_vmem_limit(vmem)),
        name="mix_outproj",
    )(x, ypl, ya, g, w_out)


def _store_col_blocks(vals, o_ref):
    cb = o_ref.shape[2]
    for b in range(o_ref.shape[0]):
        o_ref[b] = vals[:, b * cb:(b + 1) * cb]


def _cast_kernel(w_ref, o_ref):
    _store_col_blocks(w_ref[...].astype(BF16), o_ref)


def _scaled_cast_kernel(w_ref, s_ref, o_ref):
    _store_col_blocks((w_ref[...] * s_ref[...]).astype(BF16), o_ref)


def _to_bf16(w, col_block, col_scale=None):
    depth, r, c = w.shape
    rows = depth * r
    tr = rows
    while tr * c * 4 > CAST_BLOCK_BYTES:
        tr //= 2
    in_specs = [pl.BlockSpec((tr, c), lambda i: (i, 0))]
    args = [w.reshape(rows, c)]
    body = _cast_kernel
    if col_scale is not None:
        in_specs.append(pl.BlockSpec((1, c), lambda i: (0, 0)))
        args.append(col_scale.reshape(1, c))
        body = _scaled_cast_kernel
    return pl.pallas_call(
        body,
        out_shape=jax.ShapeDtypeStruct((c // col_block, rows, col_block), BF16),
        grid=(rows // tr,),
        in_specs=in_specs,
        out_specs=pl.BlockSpec((c // col_block, tr, col_block), lambda i: (0, i, 0)),
        compiler_params=pltpu.CompilerParams(
            dimension_semantics=("parallel",),
            vmem_limit_bytes=_vmem_limit(2 * tr * c * (4 + 2) + 2 * tr * c * 4)),
        name="cast_bf16",
    )(*args)


def _block_diag(w):
    heads, c, _ = w.shape
    eye = jnp.eye(heads, dtype=w.dtype)
    return (eye[:, None, :, None] * w[:, :, None, :]).reshape(heads * c, heads * c)


def kernel(x, norm_g, ffn1_w_in, ffn1_w_out, mix_w_in, mix_w_out, pool_w, pool_scale,
           lru_conv_w, lru_conv_b, lru_gate_w, lru_gate_b, lru_lambda, ffn2_w_in, ffn2_w_out):
    xs = x.reshape(SEQ, D_MODEL)
    w1_in = _to_bf16(ffn1_w_in, FFN_TF)
    w1_out = _to_bf16(ffn1_w_out, D_MODEL)
    w2_in = _to_bf16(ffn2_w_in, FFN_TF)
    w2_out = _to_bf16(ffn2_w_out, D_MODEL)
    qscale = jnp.concatenate([jnp.ones((PL_WIDTH,), F32),
                              jnp.full((ATTN_WIDTH,), ATTN_HEAD_DIM ** -0.5, F32),
                              jnp.ones((2 * ATTN_WIDTH,), F32)])
    wm_in = _to_bf16(mix_w_in, PROJ_TN, qscale)
    wm_out = _to_bf16(mix_w_out, D_MODEL)
    pw = pool_w.astype(BF16)
    for layer in range(DEPTH):
        g = norm_g[layer].reshape(6, 1, D_MODEL)
        xs = _ffn(xs, g[0], g[1], w1_in, w1_out, layer)
        zpl, qkv, qkv16 = _inproj(xs, g[2], wm_in, layer)
        ypl = _mix(zpl, pw[layer], pool_scale[layer].reshape(1, POOL_WIDTH),
                   lru_conv_w[layer], lru_conv_b[layer].reshape(1, LRU_WIDTH),
                   _block_diag(lru_gate_w[layer, 0]).astype(BF16),
                   _block_diag(lru_gate_w[layer, 1]).astype(BF16),
                   lru_gate_b[layer, 0].reshape(1, LRU_WIDTH),
                   lru_gate_b[layer, 1].reshape(1, LRU_WIDTH),
                   lru_lambda[layer].reshape(1, LRU_WIDTH))
        ya = _attention(qkv, qkv16)
        xs = _outproj(xs, ypl, ya, g[3], wm_out, layer)
        xs = _ffn(xs, g[4], g[5], w2_in, w2_out, layer)
    return xs.reshape(x.shape)
```

```python
import jax
import jax.numpy as jnp
from jax import lax
from jax.experimental import pallas as pl
from jax.experimental.pallas import tpu as pltpu

F32 = jnp.float32
BF16 = jnp.bfloat16

D_MODEL = 2048
SEQ = 16384
DEPTH = 2
D_FF = 5632
POOL_WIDTH = 512
POOL_WINDOWS = (2, 4, 8, 16)
POOL_GROUP = 128
LRU_WIDTH = 512
LRU_HEADS = 8
LRU_HEAD_DIM = 64
LRU_CONV = 4
LRU_C = 8.0
ATTN_WIDTH = 1024
ATTN_HEAD_DIM = 64
ATTN_HEADS = 16
ATTN_DILATIONS = (1, 4, 16)
ATTN_BLOCK = 128
PL_WIDTH = POOL_WIDTH + 2 * LRU_WIDTH
IN_WIDTH = PL_WIDTH + 3 * ATTN_WIDTH
NORM_EPS = 1e-6
NEG_INF = -1e30

V7X_LANES = 128
VMEM_LIMIT_CAP = 60000 * 1024

FFN_TM = 1024
FFN_TF = 512
FFN_EDGE_ROWS = 256
PROJ_TM = 1024
PROJ_TN = 768
OUT_TM = 512
OUT_ROWS = 256
MIX_T = 512
MIX_HALO = 16
DIL_PLANES = 16
DIL_STEP = 4
ATTN_TILE = DIL_PLANES * ATTN_BLOCK
ATTN_W = 256
HEAD_PAIRS = ATTN_W // V7X_LANES
ATTN_GROUP = 4
QUARTER = ATTN_BLOCK // 4
COMBINE_ROWS = 256
SCATTER_PITCH = 24
CAST_BLOCK_BYTES = 8 * 1024 * 1024


def _vmem_limit(nbytes):
    return int(min(VMEM_LIMIT_CAP, nbytes))


def _rms_scale(x):
    return lax.rsqrt(jnp.mean(x * x, axis=-1, keepdims=True) + NORM_EPS)


def _ffn_kernel(x_ref, gin_ref, gout_ref, wg_ref, wu_ref, wo_ref, o_ref, xn_ref):
    f = pl.program_id(1)
    last = pl.num_programs(1) - 1

    def swiglu_part(rows):
        xn = xn_ref[rows, :]
        gate = jnp.dot(xn, wg_ref[...], preferred_element_type=F32)
        up = jnp.dot(xn, wu_ref[...], preferred_element_type=F32)
        h = ((gate * jax.nn.sigmoid(gate)) * up).astype(BF16)
        return jnp.dot(h, wo_ref[...], preferred_element_type=F32)

    edge_chunks = [slice(c * FFN_EDGE_ROWS, (c + 1) * FFN_EDGE_ROWS)
                   for c in range(o_ref.shape[0] // FFN_EDGE_ROWS)]

    @pl.when(f == 0)
    def _():
        for rows in edge_chunks:
            scale = _rms_scale(x_ref[rows, :])
            xn_ref[rows, :] = ((x_ref[rows, :] * scale) * gin_ref[...]).astype(BF16)
            o_ref[rows, :] = swiglu_part(rows)

    @pl.when(jnp.logical_and(f > 0, f < last))
    def _():
        o_ref[...] += swiglu_part(slice(None))

    @pl.when(f == last)
    def _():
        half_g = 0.5 * gout_ref[...]
        for rows in edge_chunks:
            o_ref[rows, :] += swiglu_part(rows)
            scale = _rms_scale(o_ref[rows, :])
            o_ref[rows, :] = x_ref[rows, :] + (o_ref[rows, :] * scale) * half_g


def _ffn(x, g_in, g_out, w_in, w_out, layer):
    nf = D_FF // FFN_TF
    vmem = (4 * FFN_TM * D_MODEL * 4
            + FFN_TM * D_MODEL * 2
            + 2 * 3 * D_MODEL * FFN_TF * 2
            + 8 * FFN_TM * FFN_TF * 4)
    return pl.pallas_call(
        _ffn_kernel,
        out_shape=jax.ShapeDtypeStruct((SEQ, D_MODEL), F32),
        grid=(SEQ // FFN_TM, nf),
        in_specs=[
            pl.BlockSpec((FFN_TM, D_MODEL), lambda i, f: (i, 0)),
            pl.BlockSpec((1, D_MODEL), lambda i, f: (0, 0)),
            pl.BlockSpec((1, D_MODEL), lambda i, f: (0, 0)),
            pl.BlockSpec((None, D_MODEL, FFN_TF), lambda i, f: (f, layer, 0)),
            pl.BlockSpec((None, D_MODEL, FFN_TF), lambda i, f: (f + nf, layer, 0)),
            pl.BlockSpec((None, FFN_TF, D_MODEL), lambda i, f: (0, layer * nf + f, 0)),
        ],
        out_specs=pl.BlockSpec((FFN_TM, D_MODEL), lambda i, f: (i, 0)),
        scratch_shapes=[pltpu.VMEM((FFN_TM, D_MODEL), BF16)],
        compiler_params=pltpu.CompilerParams(
            dimension_semantics=("parallel", "arbitrary"),
            vmem_limit_bytes=_vmem_limit(vmem)),
        name="ffn",
    )(x, g_in, g_out, w_in, w_in, w_out)


PL_TILES = PL_WIDTH // PROJ_TN


def _inproj_kernel(x_ref, g_ref, w_ref, zpl_ref, qkv_ref, qkv16_ref, xn_ref, zs_ref, zt_ref):
    n = pl.program_id(1)

    @pl.when(n == 0)
    def _():
        for c in range(PROJ_TM // FFN_EDGE_ROWS):
            rows = slice(c * FFN_EDGE_ROWS, (c + 1) * FFN_EDGE_ROWS)
            scale = _rms_scale(x_ref[rows, :])
            xn_ref[rows, :] = ((x_ref[rows, :] * scale) * g_ref[...]).astype(BF16)
            zpl_ref[rows, :] = jnp.dot(xn_ref[rows, :], w_ref[...], preferred_element_type=F32)

    @pl.when(jnp.logical_and(n > 0, n < PL_TILES))
    def _():
        zpl_ref[...] = jnp.dot(xn_ref[...], w_ref[...], preferred_element_type=F32)

    @pl.when(n >= PL_TILES)
    def _():
        z = jnp.dot(xn_ref[...], w_ref[...], preferred_element_type=F32)
        _store_col_blocks(z.astype(BF16), qkv_ref)
        per_slab = ATTN_W // V7X_LANES
        for c in range(PROJ_TN // V7X_LANES):
            zs_ref[c] = z[:, c * V7X_LANES:(c + 1) * V7X_LANES]
            for r1 in range(DIL_STEP):
                zt_ref[c, r1] = zs_ref[c, pl.ds(r1, PROJ_TM // DIL_STEP, stride=DIL_STEP), :]
            lanes = slice((c % per_slab) * V7X_LANES, (c % per_slab + 1) * V7X_LANES)
            for r1 in range(DIL_STEP):
                for r2 in range(DIL_STEP):
                    plane = zt_ref[c, r1, pl.ds(r2, PROJ_TM // DIL_PLANES, stride=DIL_STEP), :]
                    qkv16_ref[c // per_slab, r1 + DIL_STEP * r2, :, lanes] = plane.astype(BF16)


def _inproj(x, g, w_in, layer):
    rows_per_plane = PROJ_TM // DIL_PLANES
    vmem = (2 * PROJ_TM * D_MODEL * 4 + PROJ_TM * D_MODEL * 2 + 2 * D_MODEL * PROJ_TN * 2
            + 2 * PROJ_TM * PROJ_TN * (4 + 2 + 2) + 4 * PROJ_TM * PROJ_TN * 4)
    qkv_col = lambda n: jnp.maximum(n - PL_TILES, 0)
    slabs = 3 * ATTN_WIDTH // ATTN_W
    slabs_per_step = PROJ_TN // ATTN_W
    return pl.pallas_call(
        _inproj_kernel,
        out_shape=(jax.ShapeDtypeStruct((SEQ, PL_WIDTH), F32),
                   jax.ShapeDtypeStruct((slabs, SEQ, ATTN_W), BF16),
                   jax.ShapeDtypeStruct((slabs, DIL_PLANES, SEQ // DIL_PLANES, ATTN_W), BF16)),
        grid=(SEQ // PROJ_TM, IN_WIDTH // PROJ_TN),
        in_specs=[
            pl.BlockSpec((PROJ_TM, D_MODEL), lambda i, n: (i, 0)),
            pl.BlockSpec((1, D_MODEL), lambda i, n: (0, 0)),
            pl.BlockSpec((None, D_MODEL, PROJ_TN), lambda i, n: (n, layer, 0)),
        ],
        out_specs=(
            pl.BlockSpec((PROJ_TM, PROJ_TN), lambda i, n: (i, jnp.minimum(n, PL_TILES - 1))),
            pl.BlockSpec((slabs_per_step, PROJ_TM, ATTN_W), lambda i, n: (qkv_col(n), i, 0)),
            pl.BlockSpec((slabs_per_step, DIL_PLANES, rows_per_plane, ATTN_W),
                         lambda i, n: (qkv_col(n), 0, i, 0)),
        ),
        scratch_shapes=[pltpu.VMEM((PROJ_TM, D_MODEL), BF16),
                        pltpu.VMEM((PROJ_TN // V7X_LANES, PROJ_TM, V7X_LANES), F32),
                        pltpu.VMEM((PROJ_TN // V7X_LANES, DIL_STEP, PROJ_TM // DIL_STEP, V7X_LANES), F32)],
        compiler_params=pltpu.CompilerParams(
            dimension_semantics=("parallel", "arbitrary"),
            vmem_limit_bytes=_vmem_limit(vmem)),
        name="mix_inproj",
    )(x, g, w_in)


def _shift_rows(v, k, fill):
    t = v.shape[0]
    if k % 8 == 0:
        pad = jnp.full((k, v.shape[1]), fill, v.dtype)
        return jnp.concatenate([pad, v[: t - k]], axis=0)
    rolled = pltpu.roll(v, k, 0)
    row = lax.broadcasted_iota(jnp.int32, v.shape, 0)
    return jnp.where(row >= k, rolled, fill)


def _mix_kernel(z_ref, pw_ref, ps_ref, cw_ref, cb_ref, wr_ref, wi_ref, br_ref, bi_ref,
                lam_ref, y_ref, ext_ref, h_ref):
    i = pl.program_id(0)
    T = MIX_T
    H = MIX_HALO

    @pl.when(i == 0)
    def _():
        ext_ref[0:H, :] = jnp.zeros((H, POOL_WIDTH + LRU_WIDTH), F32)
        h_ref[...] = jnp.zeros_like(h_ref)

    ext_ref[H:H + T, :] = z_ref[:, 0:POOL_WIDTH + LRU_WIDTH]

    pos = (i * T + 1 + lax.broadcasted_iota(jnp.int32, (T, POOL_GROUP), 0)).astype(F32)
    for g, win in enumerate(POOL_WINDOWS):
        lanes = slice(g * POOL_GROUP, (g + 1) * POOL_GROUP)
        e = ext_ref[:, lanes]
        s = e
        k = 1
        while k < win:
            s = s + pltpu.roll(s, k, 0)
            k *= 2
        u = e[H:H + T]
        pooled = s[H:H + T] / jnp.minimum(pos, float(win)) - u
        y = jnp.dot(pooled.astype(BF16), pw_ref[g], preferred_element_type=F32)
        y_ref[:, lanes] = (y * ps_ref[:, lanes]).astype(BF16)

    xl = slice(POOL_WIDTH, POOL_WIDTH + LRU_WIDTH)
    xc = cb_ref[...] + cw_ref[LRU_CONV - 1:LRU_CONV, :] * ext_ref[H:H + T, xl]
    for k in range(1, LRU_CONV):
        xc = xc + cw_ref[LRU_CONV - 1 - k:LRU_CONV - k, :] * ext_ref[H - k:H - k + T, xl]
    xcb = xc.astype(BF16)
    r = jax.nn.sigmoid(jnp.dot(xcb, wr_ref[...], preferred_element_type=F32) + br_ref[...])
    ig = jax.nn.sigmoid(jnp.dot(xcb, wi_ref[...], preferred_element_type=F32) + bi_ref[...])
    nlam = -lam_ref[...]
    softplus = jnp.maximum(nlam, 0.0) + jnp.log1p(jnp.exp(-jnp.abs(nlam)))
    log_a = (-LRU_C * r) * softplus
    a = jnp.exp(log_a)
    b = jnp.sqrt(-jnp.tanh(log_a) * (a * a + 1.0)) * (ig * xc)
    k = 1
    while k < T:
        b = a * _shift_rows(b, k, 0.0) + b
        a = a * _shift_rows(a, k, 1.0)
        k *= 2
    h = a * h_ref[0:1, :] + b
    h_ref[0:1, :] = h[T - 1:T, :]
    gate_in = z_ref[:, POOL_WIDTH + LRU_WIDTH:PL_WIDTH]
    y_ref[:, POOL_WIDTH:POOL_WIDTH + LRU_WIDTH] = (h * jax.nn.gelu(gate_in)).astype(BF16)

    ext_ref[0:H, :] = ext_ref[T:T + H, :]


def _mix(zpl, pool_w, pool_scale, conv_w, conv_b, w_r, w_i, b_r, b_i, lam):
    full = lambda shape: pl.BlockSpec(shape, lambda i: (0,) * len(shape))
    vmem = (2 * MIX_T * PL_WIDTH * 4 + 2 * MIX_T * 1024 * 2 + (MIX_T + MIX_HALO) * 1024 * 4
            + 4 * 512 * 512 * 2 + 24 * MIX_T * LRU_WIDTH * 4)
    return pl.pallas_call(
        _mix_kernel,
        out_shape=jax.ShapeDtypeStruct((SEQ, POOL_WIDTH + LRU_WIDTH), BF16),
        grid=(SEQ // MIX_T,),
        in_specs=[
            pl.BlockSpec((MIX_T, PL_WIDTH), lambda i: (i, 0)),
            full((len(POOL_WINDOWS), POOL_GROUP, POOL_GROUP)),
            full((1, POOL_WIDTH)),
            full((LRU_CONV, LRU_WIDTH)),
            full((1, LRU_WIDTH)),
            full((LRU_WIDTH, LRU_WIDTH)),
            full((LRU_WIDTH, LRU_WIDTH)),
            full((1, LRU_WIDTH)),
            full((1, LRU_WIDTH)),
            full((1, LRU_WIDTH)),
        ],
        out_specs=pl.BlockSpec((MIX_T, POOL_WIDTH + LRU_WIDTH), lambda i: (i, 0)),
        scratch_shapes=[pltpu.VMEM((MIX_T + MIX_HALO, POOL_WIDTH + LRU_WIDTH), F32),
                        pltpu.VMEM((8, LRU_WIDTH), F32)],
        compiler_params=pltpu.CompilerParams(
            dimension_semantics=("arbitrary",),
            vmem_limit_bytes=_vmem_limit(vmem)),
        name="pool_lru",
    )(zpl, pool_w, pool_scale, conv_w, conv_b, w_r, w_i, b_r, b_i, lam)


_CONTRACT_LANES = (((1,), (1,)), ((), ()))
_CONTRACT_ROWS = (((0,), (0,)), ((), ()))


def _mod_pow2(x, n):
    assert n & (n - 1) == 0
    return jnp.bitwise_and(x, n - 1)


def _div_pow2(x, n):
    assert n & (n - 1) == 0
    return jnp.right_shift(x, n.bit_length() - 1)


def _build_bias(hg, bias_s):
    B = ATTN_BLOCK
    u = lax.broadcasted_iota(jnp.int32, (2 * B, 2 * B), 0)
    col = lax.broadcasted_iota(jnp.int32, (2 * B, 2 * B), 1)
    qi = _mod_pow2(col, B)
    heads_per_step = ATTN_W // ATTN_HEAD_DIM
    step_ratio = 2.0 ** (-8.0 * heads_per_step / ATTN_HEADS)
    group_scale = jnp.float32(1.0)
    for grp in range(1, ATTN_WIDTH // ATTN_W):
        group_scale = jnp.where(hg == grp, step_ratio ** grp, group_scale)
    for bi, dil in enumerate(ATTN_DILATIONS):
        if dil == 4:
            delta = (4 * (_mod_pow2(qi, QUARTER) - _mod_pow2(u, 2 * QUARTER) + QUARTER)
                     + _div_pow2(qi, QUARTER) - _div_pow2(u, 2 * QUARTER))
            prev = _mod_pow2(u, 2 * QUARTER) < QUARTER
        else:
            delta = qi - u + B
            prev = u < B
        valid = jnp.logical_and(delta >= 0, delta <= B)
        dist = (delta * dil).astype(F32)
        for hp in range(HEAD_PAIRS):
            slope_a = 2.0 ** (-8.0 * (2 * hp + 1) / ATTN_HEADS)
            slope_b = 2.0 ** (-8.0 * (2 * hp + 2) / ATTN_HEADS)
            slope = group_scale * jnp.where(col < B, slope_a, slope_b)
            bias = (-slope) * dist
            bias_s[bi, 0, hp] = jnp.where(valid, bias, NEG_INF)
            bias_s[bi, 1, hp] = jnp.where(jnp.logical_and(valid, jnp.logical_not(prev)), bias, NEG_INF)


def _scores_stage(q_pair, kcat, bias, s_ref, m_ref, slot):
    low = lax.broadcasted_iota(jnp.int32, q_pair.shape, 1) < ATTN_HEAD_DIM
    zero = jnp.zeros_like(q_pair)
    qcat = jnp.concatenate([jnp.where(low, q_pair, zero), jnp.where(low, zero, q_pair)], axis=0)
    s = lax.dot_general(kcat, qcat, _CONTRACT_LANES, preferred_element_type=F32) + bias
    s_ref[slot] = s
    m_ref[slot, 0:1, :] = jnp.max(s, axis=0, keepdims=True)


def _probs_stage(vcat, s_ref, m_ref, ot_ref, ml_ref, slot):
    m = m_ref[slot, 0:1, :]
    p = jnp.exp(s_ref[slot] - m)
    ot_ref[slot] = lax.dot_general(vcat, p.astype(BF16), _CONTRACT_ROWS, preferred_element_type=F32)
    ml_ref[slot, 0:1, :] = m
    ml_ref[slot, 1:2, :] = jnp.sum(p, axis=0, keepdims=True)


def _finish_stage(ot_ref, ml_ref, slot):
    B = ATTN_BLOCK
    H = ATTN_HEAD_DIM
    ot = ot_ref[slot]
    m = ml_ref[slot, 0:1, :]
    l = ml_ref[slot, 1:2, :]
    inv = 1.0 / l
    lse = m + jnp.log(l)
    o_t = jnp.concatenate([ot[0:H, 0:B] * inv[:, 0:B], ot[H:2 * H, B:2 * B] * inv[:, B:2 * B]], axis=0)
    l_t = jnp.concatenate([jnp.broadcast_to(lse[:, 0:B], (H, B)),
                           jnp.broadcast_to(lse[:, B:2 * B], (H, B))], axis=0)
    return o_t.T, l_t.T


def _software_pipeline(n, scores, probs, finish):
    scores(0)
    probs(0)
    scores(1)

    def body(t, carry):
        finish(t - 2)
        probs(t - 1)
        scores(t)
        return carry
    lax.fori_loop(2, n, body, 0)
    finish(n - 2)
    probs(n - 1)
    finish(n - 1)


def _attn_kernel(qn, kn, vn, q16, k16, v16, y_ref, kxn, vxn, kx16, vx16, o_s, l_s, bias_s, *stage):
    hg = pl.program_id(0)
    j = pl.program_id(1)
    B = ATTN_BLOCK
    T = ATTN_TILE
    first = jnp.where(j == 0, 1, 0)

    @pl.when(j == 0)
    def _():
        kxn[0:B, :] = jnp.zeros((B, ATTN_W), BF16)
        vxn[0:B, :] = jnp.zeros((B, ATTN_W), BF16)
        kx16[:, 0:B, :] = jnp.zeros((DIL_PLANES, B, ATTN_W), BF16)
        vx16[:, 0:B, :] = jnp.zeros((DIL_PLANES, B, ATTN_W), BF16)
        _build_bias(hg, bias_s)

    kxn[B:B + T, :] = kn[...]
    vxn[B:B + T, :] = vn[...]
    kx16[:, B:2 * B, :] = k16[...]
    vx16[:, B:2 * B, :] = v16[...]

    def lanes_of(hp):
        return slice(hp * V7X_LANES, (hp + 1) * V7X_LANES)

    def aligned(x, m):
        return x if isinstance(x, int) else pl.multiple_of(x, m)

    def run_branch(bi, fetch_q, fetch_k, fetch_v, variant_of, scatter):
        def units(t):
            for i in range(ATTN_GROUP):
                for hp in range(HEAD_PAIRS):
                    yield t * ATTN_GROUP + i, hp, stage[4 * (i * HEAD_PAIRS + hp):4 * (i * HEAD_PAIRS + hp) + 4]

        def scores(t):
            for blk, hp, (s_ref, m_ref, _, _) in units(t):
                _scores_stage(fetch_q(blk, hp), fetch_k(blk, hp), bias_s[bi, variant_of(blk), hp],
                              s_ref, m_ref, t % 2)

        def probs(t):
            for blk, hp, (s_ref, m_ref, ot_ref, ml_ref) in units(t):
                _probs_stage(fetch_v(blk, hp), s_ref, m_ref, ot_ref, ml_ref, t % 2)

        def finish(t):
            for blk, hp, (_, _, ot_ref, ml_ref) in units(t):
                o, l = _finish_stage(ot_ref, ml_ref, t % 2)
                scatter(blk, hp, o, l)

        _software_pipeline(DIL_PLANES // ATTN_GROUP, scores, probs, finish)

    def d1_rows(blk, n):
        return pl.ds(aligned(blk * B, B), n)

    def d1_scatter(blk, hp, o, l):
        for k in range(B // DIL_PLANES):
            rows = pl.ds(aligned((blk * (B // DIL_PLANES) + k) * SCATTER_PITCH, 8), DIL_PLANES)
            o_s[0, hp, rows, :] = o[k * DIL_PLANES:(k + 1) * DIL_PLANES]
            l_s[0, hp, rows, :] = l[k * DIL_PLANES:(k + 1) * DIL_PLANES]

    run_branch(0,
               lambda blk, hp: qn[d1_rows(blk, B), lanes_of(hp)],
               lambda blk, hp: kxn[d1_rows(blk, 2 * B), lanes_of(hp)],
               lambda blk, hp: vxn[d1_rows(blk, 2 * B), lanes_of(hp)],
               lambda blk: jnp.where(blk == 0, first, 0),
               d1_scatter)

    per_plane = ATTN_BLOCK // QUARTER

    def d4_gather(ref, blk, hp, row0, n):
        c = blk // per_plane
        rows = pl.ds(aligned(row0 + (blk % per_plane) * QUARTER, QUARTER), n)
        return jnp.concatenate([ref[per_plane * a + c, rows, lanes_of(hp)] for a in range(per_plane)], axis=0)

    def d4_scatter(blk, hp, o, l):
        c = blk // per_plane
        b = blk % per_plane
        for a in range(per_plane):
            rows = pl.ds(b * QUARTER * SCATTER_PITCH + per_plane * a + c, QUARTER, stride=SCATTER_PITCH)
            o_s[1, hp, rows, :] = o[a * QUARTER:(a + 1) * QUARTER]
            l_s[1, hp, rows, :] = l[a * QUARTER:(a + 1) * QUARTER]

    run_branch(1,
               lambda blk, hp: d4_gather(q16, blk, hp, 0, QUARTER),
               lambda blk, hp: d4_gather(kx16, blk, hp, B - QUARTER, 2 * QUARTER),
               lambda blk, hp: d4_gather(vx16, blk, hp, B - QUARTER, 2 * QUARTER),
               lambda blk: jnp.where(blk % per_plane == 0, first, 0),
               d4_scatter)

    def d16_scatter(blk, hp, o, l):
        rows = pl.ds(blk, B, stride=SCATTER_PITCH)
        o_s[2, hp, rows, :] = o
        l_s[2, hp, rows, :] = l

    run_branch(2,
               lambda blk, hp: q16[blk, :, lanes_of(hp)],
               lambda blk, hp: kx16[blk, :, lanes_of(hp)],
               lambda blk, hp: vx16[blk, :, lanes_of(hp)],
               lambda blk: first,
               d16_scatter)

    def merge(c, carry):
        rows = pl.ds(pl.multiple_of(c * COMBINE_ROWS, COMBINE_ROWS), COMBINE_ROWS)
        groups = COMBINE_ROWS // DIL_PLANES

        def packed(ref, bi, hp):
            return jnp.concatenate(
                [ref[bi, hp, pl.ds(pl.multiple_of((c * groups + i) * SCATTER_PITCH, 8), DIL_PLANES), :]
                 for i in range(groups)], axis=0)

        for hp in range(HEAD_PAIRS):
            l1 = packed(l_s, 0, hp)
            l4 = packed(l_s, 1, hp)
            l16 = packed(l_s, 2, hp)
            top = jnp.maximum(jnp.maximum(l1, l4), l16)
            w1 = jnp.exp(l1 - top)
            w4 = jnp.exp(l4 - top)
            w16 = jnp.exp(l16 - top)
            num = w1 * packed(o_s, 0, hp) + w4 * packed(o_s, 1, hp) + w16 * packed(o_s, 2, hp)
            y_ref[rows, lanes_of(hp)] = (num / (w1 + w4 + w16)).astype(BF16)
        return carry
    lax.fori_loop(0, T // COMBINE_ROWS, merge, 0)

    kxn[0:B, :] = kxn[T:T + B, :]
    vxn[0:B, :] = vxn[T:T + B, :]
    kx16[:, 0:B, :] = kx16[:, B:2 * B, :]
    vx16[:, 0:B, :] = vx16[:, B:2 * B, :]


def _attention(qkv, qkv16):
    B = ATTN_BLOCK
    ngrp = ATTN_WIDTH // ATTN_W
    nat = lambda which: pl.BlockSpec((None, ATTN_TILE, ATTN_W), lambda hg, j: (which * ngrp + hg, j, 0))
    planes = lambda which: pl.BlockSpec((None, DIL_PLANES, B, ATTN_W),
                                        lambda hg, j: (which * ngrp + hg, 0, j, 0))
    vmem = (2 * 6 * ATTN_TILE * ATTN_W * 2 + 2 * ATTN_TILE * ATTN_W * 2
            + 2 * (ATTN_TILE + B) * ATTN_W * 2 + 2 * DIL_PLANES * 2 * B * ATTN_W * 2
            + 2 * 3 * HEAD_PAIRS * ATTN_BLOCK * SCATTER_PITCH * V7X_LANES * 4
            + 3 * 2 * HEAD_PAIRS * 4 * B * B * 4
            + 16 * 1024 * 1024)
    return pl.pallas_call(
        _attn_kernel,
        out_shape=jax.ShapeDtypeStruct((ngrp, SEQ, ATTN_W), BF16),
        grid=(ngrp, SEQ // ATTN_TILE),
        in_specs=[nat(0), nat(1), nat(2), planes(0), planes(1), planes(2)],
        out_specs=pl.BlockSpec((None, ATTN_TILE, ATTN_W), lambda hg, j: (hg, j, 0)),
        scratch_shapes=[
            pltpu.VMEM((ATTN_TILE + B, ATTN_W), BF16),
            pltpu.VMEM((ATTN_TILE + B, ATTN_W), BF16),
            pltpu.VMEM((DIL_PLANES, 2 * B, ATTN_W), BF16),
            pltpu.VMEM((DIL_PLANES, 2 * B, ATTN_W), BF16),
            pltpu.VMEM((3, HEAD_PAIRS, ATTN_BLOCK * SCATTER_PITCH, V7X_LANES), F32),
            pltpu.VMEM((3, HEAD_PAIRS, ATTN_BLOCK * SCATTER_PITCH, V7X_LANES), F32),
            pltpu.VMEM((3, 2, HEAD_PAIRS, 2 * B, 2 * B), F32),
        ] + [
            pltpu.VMEM(shape, F32)
            for _ in range(ATTN_GROUP * HEAD_PAIRS)
            for shape in ((2, 2 * B, 2 * B), (2, 8, 2 * B), (2, B, 2 * B), (2, 8, 2 * B))
        ],
        compiler_params=pltpu.CompilerParams(
            dimension_semantics=("arbitrary", "arbitrary"),
            vmem_limit_bytes=_vmem_limit(vmem)),
        name="dilated_attn",
    )(qkv, qkv, qkv, qkv16, qkv16, qkv16)


def _outproj_kernel(x_ref, ypl_ref, ya_ref, g_ref, w_ref, o_ref):
    half = POOL_WIDTH + LRU_WIDTH

    for c in range(o_ref.shape[0] // OUT_ROWS):
        rows = slice(c * OUT_ROWS, (c + 1) * OUT_ROWS)
        h = jnp.dot(ypl_ref[rows, :], w_ref[0:half, :], preferred_element_type=F32)
        for grp in range(ya_ref.shape[0]):
            w_rows = slice(half + grp * ATTN_W, half + (grp + 1) * ATTN_W)
            h = h + jnp.dot(ya_ref[grp, rows, :], w_ref[w_rows, :], preferred_element_type=F32)
        o_ref[rows, :] = x_ref[rows, :] + (h * _rms_scale(h)) * g_ref[...]


def _outproj(x, ypl, ya, g, w_out, layer):
    half = POOL_WIDTH + LRU_WIDTH
    vmem = (4 * OUT_TM * D_MODEL * 4 + 4 * OUT_TM * half * 2 + 2 * D_MODEL * D_MODEL * 2
            + 4 * OUT_TM * D_MODEL * 4)
    return pl.pallas_call(
        _outproj_kernel,
        out_shape=jax.ShapeDtypeStruct((SEQ, D_MODEL), F32),
        grid=(SEQ // OUT_TM,),
        in_specs=[
            pl.BlockSpec((OUT_TM, D_MODEL), lambda i: (i, 0)),
            pl.BlockSpec((OUT_TM, half), lambda i: (i, 0)),
            pl.BlockSpec((ATTN_WIDTH // ATTN_W, OUT_TM, ATTN_W), lambda i: (0, i, 0)),
            pl.BlockSpec((1, D_MODEL), lambda i: (0, 0)),
            pl.BlockSpec((None, D_MODEL, D_MODEL), lambda i: (0, layer, 0)),
        ],
        out_specs=pl.BlockSpec((OUT_TM, D_MODEL), lambda i: (i, 0)),
        compiler_params=pltpu.CompilerParams(
            dimension_semantics=("parallel",),
            vmem_limit_bytes=_vmem_limit(vmem)),
        name="mix_outproj",
    )(x, ypl, ya, g, w_out)


def _store_col_blocks(vals, o_ref):
    cb = o_ref.shape[2]
    for b in range(o_ref.shape[0]):
        o_ref[b] = vals[:, b * cb:(b + 1) * cb]


def _cast_kernel(w_ref, o_ref):
    _store_col_blocks(w_ref[...].astype(BF16), o_ref)


def _scaled_cast_kernel(w_ref, s_ref, o_ref):
    _store_col_blocks((w_ref[...] * s_ref[...]).astype(BF16), o_ref)


def _to_bf16(w, col_block, col_scale=None):
    depth, r, c = w.shape
    rows = depth * r
    tr = rows
    while tr * c * 4 > CAST_BLOCK_BYTES:
        tr //= 2
    in_specs = [pl.BlockSpec((tr, c), lambda i: (i, 0))]
    args = [w.reshape(rows, c)]
    body = _cast_kernel
    if col_scale is not None:
        in_specs.append(pl.BlockSpec((1, c), lambda i: (0, 0)))
        args.append(col_scale.reshape(1, c))
        body = _scaled_cast_kernel
    return pl.pallas_call(
        body,
        out_shape=jax.ShapeDtypeStruct((c // col_block, rows, col_block), BF16),
        grid=(rows // tr,),
        in_specs=in_specs,
        out_specs=pl.BlockSpec((c // col_block, tr, col_block), lambda i: (0, i, 0)),
        compiler_params=pltpu.CompilerParams(
            dimension_semantics=("parallel",),
            vmem_limit_bytes=_vmem_limit(2 * tr * c * (4 + 2) + 2 * tr * c * 4)),
        name="cast_bf16",
    )(*args)


def _block_diag(w):
    heads, c, _ = w.shape
    eye = jnp.eye(heads, dtype=w.dtype)
    return (eye[:, None, :, None] * w[:, :, None, :]).reshape(heads * c, heads * c)


def kernel(x, norm_g, ffn1_w_in, ffn1_w_out, mix_w_in, mix_w_out, pool_w, pool_scale,
           lru_conv_w, lru_conv_b, lru_gate_w, lru_gate_b, lru_lambda, ffn2_w_in, ffn2_w_out):
    xs = x.reshape(SEQ, D_MODEL)
    w1_in = _to_bf16(ffn1_w_in, FFN_TF)
    w1_out = _to_bf16(ffn1_w_out, D_MODEL)
    w2_in = _to_bf16(ffn2_w_in, FFN_TF)
    w2_out = _to_bf16(ffn2_w_out, D_MODEL)
    qscale = jnp.concatenate([jnp.ones((PL_WIDTH,), F32),
                              jnp.full((ATTN_WIDTH,), ATTN_HEAD_DIM ** -0.5, F32),
                              jnp.ones((2 * ATTN_WIDTH,), F32)])
    wm_in = _to_bf16(mix_w_in, PROJ_TN, qscale)
    wm_out = _to_bf16(mix_w_out, D_MODEL)
    pw = pool_w.astype(BF16)
    for layer in range(DEPTH):
        g = norm_g[layer].reshape(6, 1, D_MODEL)
        xs = _ffn(xs, g[0], g[1], w1_in, w1_out, layer)
        zpl, qkv, qkv16 = _inproj(xs, g[2], wm_in, layer)
        ypl = _mix(zpl, pw[layer], pool_scale[layer].reshape(1, POOL_WIDTH),
                   lru_conv_w[layer], lru_conv_b[layer].reshape(1, LRU_WIDTH),
                   _block_diag(lru_gate_w[layer, 0]).astype(BF16),
                   _block_diag(lru_gate_w[layer, 1]).astype(BF16),
                   lru_gate_b[layer, 0].reshape(1, LRU_WIDTH),
                   lru_gate_b[layer, 1].reshape(1, LRU_WIDTH),
                   lru_lambda[layer].reshape(1, LRU_WIDTH))
        ya = _attention(qkv, qkv16)
        xs = _outproj(xs, ypl, ya, g[3], wm_out, layer)
        xs = _ffn(xs, g[4], g[5], w2_in, w2_out, layer)
    return xs.reshape(x.shape)
```
